```python
import math
import jax, jax.numpy as jnp
from jax import lax
import numpy as np

D_MODEL = 4096
BATCH = 4
SEQ = 2048
DEPTH = 2
DEC_BATCH = 8
DEC_SEQ = 8
PAST_LEN = 16384
PAGE_SIZE = 128

FOX_HEADS = 12
FOX_HEAD_DIM = 128
DIFF_HEADS = 8
DIFF_QK_DIM = 64
DIFF_V_DIM = 2 * DIFF_QK_DIM
SB_HEADS = 12
SB_HEAD_DIM = 128
N_BRANCH = 3
N_ADA = 6
D_FF = 4 * D_MODEL
Q_BLOCK = 128
NORM_EPS = 1e-6

FOX_W = FOX_HEADS * FOX_HEAD_DIM
DIFF_QK_W = DIFF_HEADS * 2 * DIFF_QK_DIM
DIFF_V_W = DIFF_HEADS * DIFF_V_DIM
SB_W = SB_HEADS * SB_HEAD_DIM
IN_SPLITS = (FOX_W, FOX_W, FOX_W, FOX_HEADS, DIFF_QK_W, DIFF_QK_W, DIFF_V_W, SB_W, SB_W, SB_W, N_BRANCH * D_MODEL)
N_IN = 3 * FOX_W + FOX_HEADS + 2 * DIFF_QK_W + DIFF_V_W + 3 * SB_W + N_BRANCH * D_MODEL

kernel_name = "gated_hybrid_fox_diff_stickbreak_decode_step"


def rms_norm(x, g):
    xf = x.astype(jnp.float32)
    y = xf * lax.rsqrt(jnp.mean(xf * xf, axis=-1, keepdims=True) + NORM_EPS)
    return (y * g.astype(jnp.float32)).astype(x.dtype)


def alibi_slopes(n_heads):
    return (2.0 ** (-8.0 * (np.arange(n_heads) + 1) / n_heads)).astype(np.float32)


def rev_excl_cumsum(lf):
    lf = lf.astype(jnp.float32)
    return lax.cumsum(lf, axis=1, reverse=True) - lf


def fox_attend(q, e_q, q_pos, k, v, e_k, k_pos):
    s = jnp.einsum('bthd,bshd->bhts', q, k, preferred_element_type=jnp.float32) * (1.0 / math.sqrt(FOX_HEAD_DIM))
    s = s + jnp.transpose(e_k, (0, 2, 1))[:, :, None, :] - jnp.transpose(e_q, (0, 2, 1))[:, :, :, None]
    causal = k_pos[None, :] <= q_pos[:, None]
    p = jax.nn.softmax(jnp.where(causal, s, -jnp.inf), axis=-1)
    return jnp.einsum('bhts,bshd->bthd', p.astype(v.dtype), v)


def diff_attend(q, q_pos, k, v, k_pos, lam, slopes):
    s = jnp.einsum('bthmd,bshmd->bhmts', q, k, preferred_element_type=jnp.float32) * (1.0 / math.sqrt(DIFF_QK_DIM))
    dist = jnp.abs(q_pos[:, None] - k_pos[None, :]).astype(jnp.float32)
    s = s - slopes[None, :, None, None, None] * dist
    causal = k_pos[None, :] <= q_pos[:, None]
    p = jax.nn.softmax(jnp.where(causal, s, -jnp.inf), axis=-1)
    a = p[:, :, 0] - lam * p[:, :, 1]
    return jnp.einsum('bhts,bshe->bthe', a.astype(v.dtype), v)


def sb_attend(q, q_pos, k, v, k_pos):
    z = jnp.einsum('bthd,bshd->bhts', q, k, preferred_element_type=jnp.float32) * (1.0 / math.sqrt(SB_HEAD_DIM))
    strict = k_pos[None, :] < q_pos[:, None]
    log_1mb = jnp.where(strict, jax.nn.log_sigmoid(-z), 0.0)
    key_axis = log_1mb.ndim - 1
    after = lax.cumsum(log_1mb, axis=key_axis, reverse=True) - log_1mb
    a = jnp.exp(jnp.where(strict, jax.nn.log_sigmoid(z) + after, -jnp.inf))
    return jnp.einsum('bhts,bshd->bthd', a.astype(v.dtype), v)


def sweep_query_blocks(fn, q_items, seq):
    n_blocks = seq // Q_BLOCK

    def body(i):
        s0 = i * Q_BLOCK
        blk = [lax.dynamic_slice_in_dim(a, s0, Q_BLOCK, axis=1) for a in q_items]
        return fn(*blk, s0 + jnp.arange(Q_BLOCK, dtype=jnp.int32))

    out = lax.map(body, jnp.arange(n_blocks, dtype=jnp.int32))
    out = jnp.moveaxis(out, 0, 1)
    return out.reshape((out.shape[0], seq) + out.shape[3:])


def modulation(c, w_ada, b_ada):
    ada = jnp.einsum('bd,de->be', jax.nn.silu(c), w_ada) + b_ada
    ada = ada.reshape(c.shape[0], N_ADA, D_MODEL)
    return tuple(ada[:, i] for i in range(N_ADA))


def pre_mixer(x, mod, g_norm1, w_in, b_fgate, g_q_fox, g_k_fox, g_q_diff, g_k_diff):
    shift1, scale1 = mod[0], mod[1]
    b, t, _ = x.shape
    h = rms_norm(x, g_norm1) * (1.0 + scale1[:, None, :]) + shift1[:, None, :]
    u = jnp.einsum('btd,dn->btn', h, w_in)
    idx = np.cumsum(IN_SPLITS)[:-1].tolist()
    qa, ka, va, fa, qb, kb, vb, qc, kc, vc, gl = jnp.split(u, idx, axis=-1)
    qa = rms_norm(qa.reshape(b, t, FOX_HEADS, FOX_HEAD_DIM), g_q_fox)
    ka = rms_norm(ka.reshape(b, t, FOX_HEADS, FOX_HEAD_DIM), g_k_fox)
    va = va.reshape(b, t, FOX_HEADS, FOX_HEAD_DIM)
    lf = jax.nn.log_sigmoid((fa + b_fgate).astype(jnp.float32))
    qb = rms_norm(qb.reshape(b, t, DIFF_HEADS, 2, DIFF_QK_DIM), g_q_diff)
    kb = rms_norm(kb.reshape(b, t, DIFF_HEADS, 2, DIFF_QK_DIM), g_k_diff)
    vb = vb.reshape(b, t, DIFF_HEADS, DIFF_V_DIM)
    qc = qc.reshape(b, t, SB_HEADS, SB_HEAD_DIM)
    kc = kc.reshape(b, t, SB_HEADS, SB_HEAD_DIM)
    vc = vc.reshape(b, t, SB_HEADS, SB_HEAD_DIM)
    gates = jax.nn.sigmoid(gl.reshape(b, t, N_BRANCH, D_MODEL).astype(jnp.float32)).astype(x.dtype)
    return qa, ka, va, lf, qb, kb, vb, qc, kc, vc, gates


def post_mixer(x, mod, oa, ob, oc, gates, lam_init, g_sub_diff, w_br_fox, w_br_diff, w_br_sb, w_out, g_norm2, w_up, w_down):
    gate1, shift2, scale2, gate2 = mod[2], mod[3], mod[4], mod[5]
    b, t, _ = x.shape
    ob = rms_norm(ob, g_sub_diff) * (1.0 - lam_init)
    pa = jnp.einsum('btf,fd->btd', oa.reshape(b, t, FOX_W), w_br_fox)
    pb = jnp.einsum('btf,fd->btd', ob.reshape(b, t, DIFF_V_W), w_br_diff)
    pc = jnp.einsum('btf,fd->btd', oc.reshape(b, t, SB_W), w_br_sb)
    merged = gates[:, :, 0] * pa + gates[:, :, 1] * pb + gates[:, :, 2] * pc
    x = x + gate1[:, None, :] * jnp.einsum('btd,de->bte', merged, w_out)
    h = rms_norm(x, g_norm2) * (1.0 + scale2[:, None, :]) + shift2[:, None, :]
    f = jnp.square(jax.nn.relu(jnp.einsum('btd,df->btf', h, w_up)))
    return x + gate2[:, None, :] * jnp.einsum('btf,fd->btd', f, w_down)


def gather_pages(cache, l, page_table):
    g = cache[l, page_table]
    return g.reshape((g.shape[0], g.shape[1] * g.shape[2]) + g.shape[3:])


def setup_inputs(seed: int = 0) -> dict:
    key = jax.random.key(seed)
    ks = iter(jax.random.split(key, 40))

    def nrm(shape, scale):
        return jax.random.normal(next(ks), shape, jnp.float32) * scale

    n_pages = PAST_LEN // PAGE_SIZE
    n_used = DEC_BATCH * n_pages
    n_pool = n_used + max(1, n_used // 4)
    x_prompt = nrm((BATCH, SEQ, D_MODEL), 1.0)
    x_sample = nrm((DEC_BATCH, DEC_SEQ, D_MODEL), 1.0)
    cache_fox_k = nrm((DEPTH, n_pool, PAGE_SIZE, FOX_HEADS, FOX_HEAD_DIM), 1.0)
    cache_fox_v = nrm((DEPTH, n_pool, PAGE_SIZE, FOX_HEADS, FOX_HEAD_DIM), 1.0)
    cache_fox_logf = jax.nn.log_sigmoid(nrm((DEPTH, n_pool, PAGE_SIZE, FOX_HEADS), 1.0))
    cache_diff_k = nrm((DEPTH, n_pool, PAGE_SIZE, DIFF_HEADS, 2, DIFF_QK_DIM), 1.0)
    cache_diff_v = nrm((DEPTH, n_pool, PAGE_SIZE, DIFF_HEADS, DIFF_V_DIM), 1.0)
    cache_sb_k = nrm((DEPTH, n_pool, PAGE_SIZE, SB_HEADS, SB_HEAD_DIM), 1.0)
    cache_sb_v = nrm((DEPTH, n_pool, PAGE_SIZE, SB_HEADS, SB_HEAD_DIM), 1.0)
    page_table = jax.random.permutation(next(ks), n_pool)[:n_used].reshape(DEC_BATCH, n_pages).astype(jnp.int32)
    c_prompt = nrm((BATCH, D_MODEL), 1.0)
    c_sample = nrm((DEC_BATCH, D_MODEL), 1.0)
    dsc = D_MODEL ** -0.5
    return {
        "x_prompt": x_prompt, "x_sample": x_sample,
        "cache_fox_k": cache_fox_k, "cache_fox_v": cache_fox_v, "cache_fox_logf": cache_fox_logf,
        "cache_diff_k": cache_diff_k, "cache_diff_v": cache_diff_v,
        "cache_sb_k": cache_sb_k, "cache_sb_v": cache_sb_v,
        "page_table": page_table, "c_prompt": c_prompt, "c_sample": c_sample,
        "w_ada": nrm((DEPTH, D_MODEL, N_ADA * D_MODEL), 0.5 * dsc),
        "b_ada": nrm((DEPTH, N_ADA * D_MODEL), 0.02),
        "g_norm1": 1.0 + nrm((DEPTH, D_MODEL), 0.01),
        "w_in": nrm((DEPTH, D_MODEL, N_IN), dsc),
        "b_fgate": nrm((DEPTH, FOX_HEADS), 0.1),
        "g_q_fox": 1.0 + nrm((DEPTH, FOX_HEAD_DIM), 0.01),
        "g_k_fox": 1.0 + nrm((DEPTH, FOX_HEAD_DIM), 0.01),
        "g_q_diff": 1.0 + nrm((DEPTH, DIFF_QK_DIM), 0.01),
        "g_k_diff": 1.0 + nrm((DEPTH, DIFF_QK_DIM), 0.01),
        "lambda_q1": nrm((DEPTH, DIFF_QK_DIM), 0.1),
        "lambda_k1": nrm((DEPTH, DIFF_QK_DIM), 0.1),
        "lambda_q2": nrm((DEPTH, DIFF_QK_DIM), 0.1),
        "lambda_k2": nrm((DEPTH, DIFF_QK_DIM), 0.1),
        "g_sub_diff": 1.0 + nrm((DEPTH, DIFF_V_DIM), 0.01),
        "w_br_fox": nrm((DEPTH, FOX_W, D_MODEL), FOX_W ** -0.5),
        "w_br_diff": nrm((DEPTH, DIFF_V_W, D_MODEL), DIFF_V_W ** -0.5),
        "w_br_sb": nrm((DEPTH, SB_W, D_MODEL), SB_W ** -0.5),
        "w_out": nrm((DEPTH, D_MODEL, D_MODEL), dsc),
        "g_norm2": 1.0 + nrm((DEPTH, D_MODEL), 0.01),
        "w_up": nrm((DEPTH, D_MODEL, D_FF), dsc),
        "w_down": nrm((DEPTH, D_FF, D_MODEL), D_FF ** -0.5),
    }


def reference(x_prompt, x_sample, cache_fox_k, cache_fox_v, cache_fox_logf, cache_diff_k, cache_diff_v, cache_sb_k, cache_sb_v, page_table, c_prompt, c_sample, w_ada, b_ada, g_norm1, w_in, b_fgate, g_q_fox, g_k_fox, g_q_diff, g_k_diff, lambda_q1, lambda_k1, lambda_q2, lambda_k2, g_sub_diff, w_br_fox, w_br_diff, w_br_sb, w_out, g_norm2, w_up, w_down):
    slopes = jnp.asarray(alibi_slopes(DIFF_HEADS))
    yp, ys = x_prompt, x_sample
    p_fk, p_fv, p_fl, p_dk, p_dv, p_sk, p_sv = [], [], [], [], [], [], []
    s_fk, s_fv, s_fl, s_dk, s_dv, s_sk, s_sv = [], [], [], [], [], [], []
    for l in range(DEPTH):
        lam_init = 0.8 - 0.6 * math.exp(-0.3 * l)
        lam = (jnp.exp(jnp.sum(lambda_q1[l].astype(jnp.float32) * lambda_k1[l].astype(jnp.float32)))
               - jnp.exp(jnp.sum(lambda_q2[l].astype(jnp.float32) * lambda_k2[l].astype(jnp.float32))) + lam_init)

        mod_p = modulation(c_prompt, w_ada[l], b_ada[l])
        qa, ka, va, lf, qb, kb, vb, qc, kc, vc, gates = pre_mixer(
            yp, mod_p, g_norm1[l], w_in[l], b_fgate[l], g_q_fox[l], g_k_fox[l], g_q_diff[l], g_k_diff[l])
        t = yp.shape[1]
        kpos = jnp.arange(t, dtype=jnp.int32)
        ea = rev_excl_cumsum(lf)
        oa = sweep_query_blocks(lambda q_, e_, qp: fox_attend(q_, e_, qp, ka, va, ea, kpos), (qa, ea), t)
        ob = sweep_query_blocks(lambda q_, qp: diff_attend(q_, qp, kb, vb, kpos, lam, slopes), (qb,), t)
        oc = sweep_query_blocks(lambda q_, qp: sb_attend(q_, qp, kc, vc, kpos), (qc,), t)
        yp = post_mixer(yp, mod_p, oa, ob, oc, gates, lam_init, g_sub_diff[l], w_br_fox[l], w_br_diff[l],
                        w_br_sb[l], w_out[l], g_norm2[l], w_up[l], w_down[l])
        p_fk.append(ka); p_fv.append(va); p_fl.append(lf)
        p_dk.append(kb); p_dv.append(vb); p_sk.append(kc); p_sv.append(vc)

        mod_s = modulation(c_sample, w_ada[l], b_ada[l])
        qa, ka, va, lf, qb, kb, vb, qc, kc, vc, gates = pre_mixer(
            ys, mod_s, g_norm1[l], w_in[l], b_fgate[l], g_q_fox[l], g_k_fox[l], g_q_diff[l], g_k_diff[l])
        t = ys.shape[1]
        past = page_table.shape[1] * PAGE_SIZE
        q_pos = past + jnp.arange(t, dtype=jnp.int32)
        k_pos = jnp.arange(past + t, dtype=jnp.int32)
        ka_all = jnp.concatenate([gather_pages(cache_fox_k, l, page_table), ka], axis=1)
        va_all = jnp.concatenate([gather_pages(cache_fox_v, l, page_table), va], axis=1)
        lf_all = jnp.concatenate([gather_pages(cache_fox_logf, l, page_table).astype(jnp.float32), lf], axis=1)
        e_all = rev_excl_cumsum(lf_all)
        oa = fox_attend(qa, e_all[:, past:], q_pos, ka_all, va_all, e_all, k_pos)
        kb_all = jnp.concatenate([gather_pages(cache_diff_k, l, page_table), kb], axis=1)
        vb_all = jnp.concatenate([gather_pages(cache_diff_v, l, page_table), vb], axis=1)
        ob = diff_attend(qb, q_pos, kb_all, vb_all, k_pos, lam, slopes)
        kc_all = jnp.concatenate([gather_pages(cache_sb_k, l, page_table), kc], axis=1)
        vc_all = jnp.concatenate([gather_pages(cache_sb_v, l, page_table), vc], axis=1)
        oc = sb_attend(qc, q_pos, kc_all, vc_all, k_pos)
        ys = post_mixer(ys, mod_s, oa, ob, oc, gates, lam_init, g_sub_diff[l], w_br_fox[l], w_br_diff[l],
                        w_br_sb[l], w_out[l], g_norm2[l], w_up[l], w_down[l])
        s_fk.append(ka); s_fv.append(va); s_fl.append(lf)
        s_dk.append(kb); s_dv.append(vb); s_sk.append(kc); s_sv.append(vc)

    return (yp, ys,
            jnp.stack(p_fk), jnp.stack(p_fv), jnp.stack(p_fl), jnp.stack(p_dk), jnp.stack(p_dv), jnp.stack(p_sk), jnp.stack(p_sv),
            jnp.stack(s_fk), jnp.stack(s_fv), jnp.stack(s_fl), jnp.stack(s_dk), jnp.stack(s_dv), jnp.stack(s_sk), jnp.stack(s_sv))
```

```python
import functools
import math
from typing import NamedTuple

import jax
import jax.numpy as jnp
from jax import lax
from jax.experimental import pallas as pl
from jax.experimental.pallas import tpu as pltpu

F32 = jnp.float32
BF16 = jnp.bfloat16
NORM_EPS = 1e-6
LANES = 128
SUBLANES = 8
HEAD_PAD = 16
NT_DIMS = (((1,), (1,)), ((), ()))
MIB = 1 << 20
NEG_INF = float("-inf")


class Group(NamedTuple):
    B: int
    T: int
    tm: int
    per_row_mod: bool


def _pick(dim, pref):
    t = min(pref, dim)
    while t >= LANES:
        if dim % t == 0 and t % LANES == 0:
            return t
        t -= LANES
    return dim


def _cparams(n_axes, vmem_mib=None):
    return pltpu.CompilerParams(
        dimension_semantics=("arbitrary",) * n_axes,
        vmem_limit_bytes=None if vmem_mib is None else vmem_mib * MIB)


def _log_sigmoid(x):
    return jnp.minimum(x, 0.0) - jnp.log(1.0 + jnp.exp(-jnp.abs(x)))


def _split3(x):
    hi = x.astype(BF16)
    r1 = x - hi.astype(F32)
    mid = r1.astype(BF16)
    lo = (r1 - mid.astype(F32)).astype(BF16)
    return hi, mid, lo


def _suffix_ones(n):
    r = lax.broadcasted_iota(jnp.int32, (n, n), 0)
    c = lax.broadcasted_iota(jnp.int32, (n, n), 1)
    return (r > c).astype(BF16)


def _mod_spec(g, chunk, d_model, tn):
    nj = d_model // tn
    if g.per_row_mod:
        return pl.BlockSpec((g.tm, tn), lambda i, j, *_: (i, chunk * nj + j))
    nt = g.T // g.tm
    return pl.BlockSpec((None, 1, tn), lambda i, j, *_: (i // nt, 0, chunk * nj + j))


def _ada_kernel(c_ref, w_ref, b_ref, o_ref):
    c = c_ref[...]
    a = (c * jax.nn.sigmoid(c)).astype(BF16)
    o_ref[...] = jnp.dot(a, w_ref[...].astype(BF16), preferred_element_type=F32) + b_ref[...]


def _ada(c_all, w_ada, b_ada):
    n_layers, d_model, n = w_ada.shape
    rows = c_all.shape[0]
    tn = _pick(n, 512)
    return pl.pallas_call(
        _ada_kernel,
        grid=(n_layers, n // tn),
        in_specs=[pl.BlockSpec((rows, d_model), lambda l, j: (0, 0)),
                  pl.BlockSpec((None, d_model, tn), lambda l, j: (l, 0, j)),
                  pl.BlockSpec((None, 1, tn), lambda l, j: (l, 0, j))],
        out_specs=pl.BlockSpec((None, rows, tn), lambda l, j: (l, 0, j)),
        out_shape=jax.ShapeDtypeStruct((n_layers, rows, n), F32),
        compiler_params=_cparams(2, 40),
    )(c_all, w_ada, b_ada.reshape(n_layers, 1, n))


def _prenorm_kernel(x_ref, g_ref, sc_ref, sh_ref, o_ref):
    x = x_ref[...]
    ms = jnp.mean(x * x, axis=-1, keepdims=True)
    y = x * lax.rsqrt(ms + NORM_EPS) * g_ref[...]
    o_ref[...] = (y * (1.0 + sc_ref[...]) + sh_ref[...]).astype(o_ref.dtype)


def _prenorm(x, gain, mod, g, scale_chunk, shift_chunk):
    m, d_model = x.shape
    tm = min(g.tm, 256)
    gg = g._replace(tm=tm)
    return pl.pallas_call(
        _prenorm_kernel,
        grid=(m // tm, 1),
        in_specs=[pl.BlockSpec((tm, d_model), lambda i, j: (i, 0)),
                  pl.BlockSpec((1, d_model), lambda i, j: (0, 0)),
                  _mod_spec(gg, scale_chunk, d_model, d_model),
                  _mod_spec(gg, shift_chunk, d_model, d_model)],
        out_specs=pl.BlockSpec((tm, d_model), lambda i, j: (i, 0)),
        out_shape=jax.ShapeDtypeStruct((m, d_model), BF16),
        compiler_params=_cparams(2, 40),
    )(x, gain.reshape(1, d_model), mod, mod)


def _mm_kernel(*refs, nk, n_extra, epilogue):
    a_ref, w_ref = refs[0], refs[1]
    extras = refs[2:2 + n_extra]
    part = jnp.dot(a_ref[...], w_ref[...].astype(BF16), preferred_element_type=F32)
    if nk == 1:
        epilogue(part, extras, refs[2 + n_extra:])
        return
    outs, acc_ref = refs[2 + n_extra:-1], refs[-1]
    k = pl.program_id(2)

    @pl.when(k == 0)
    def _():
        acc_ref[...] = part

    @pl.when(k > 0)
    def _():
        acc_ref[...] += part

    @pl.when(k == nk - 1)
    def _():
        epilogue(acc_ref[...], extras, outs)


def _mm(a, w, col0, ncols, tm, tn, tk, epilogue, extras, extra_specs, out_shape, out_specs, vmem_mib=48):
    m, kdim = a.shape
    assert col0 % tn == 0 and ncols % tn == 0 and m % tm == 0 and kdim % tk == 0
    nk, c0 = kdim // tk, col0 // tn
    kern = functools.partial(_mm_kernel, nk=nk, n_extra=len(extras), epilogue=epilogue)
    return pl.pallas_call(
        kern,
        grid=(m // tm, ncols // tn, nk),
        in_specs=[pl.BlockSpec((tm, tk), lambda i, j, k: (i, k)),
                  pl.BlockSpec((tk, tn), lambda i, j, k: (k, c0 + j))] + list(extra_specs),
        out_specs=out_specs,
        out_shape=out_shape,
        scratch_shapes=[pltpu.VMEM((tm, tn), F32)] if nk > 1 else [],
        compiler_params=_cparams(3, vmem_mib),
    )(a, w, *extras)


def _store_heads(val, o_ref, g):
    for h in range(val.shape[1] // LANES):
        piece = val[:, h * LANES:(h + 1) * LANES].astype(o_ref.dtype)
        if g.per_row_mod:
            o_ref[:, h] = piece.reshape(g.B, g.T, LANES)
        else:
            o_ref[h] = piece


def _heads_out(g, n_heads, nh_tile, dtype):
    shape = jax.ShapeDtypeStruct((g.B, n_heads, g.T, LANES), dtype)
    if g.per_row_mod:
        return shape, pl.BlockSpec((g.B, nh_tile, g.T, LANES), lambda i, j, k: (0, j, 0, 0))
    nt = g.T // g.tm
    return shape, pl.BlockSpec((None, nh_tile, g.tm, LANES), lambda i, j, k: (i // nt, j, i % nt, 0))


def _epi_heads_plain(acc, extras, outs, *, g, scale):
    _store_heads(acc if scale == 1.0 else acc * scale, outs[0], g)


def _epi_heads_rms(acc, extras, outs, *, g, scale):
    gain = extras[0][...]
    pieces = []
    for h in range(acc.shape[1] // LANES):
        x = acc[:, h * LANES:(h + 1) * LANES]
        y = x * lax.rsqrt(jnp.mean(x * x, axis=-1, keepdims=True) + NORM_EPS) * gain
        pieces.append(y if scale == 1.0 else y * scale)
    _store_heads(jnp.concatenate(pieces, axis=1), outs[0], g)


def _epi_rms_half(acc, extras, outs, *, scale):
    gain = extras[0][...]
    lo = lax.broadcasted_iota(jnp.int32, (1, LANES), 1) < (LANES // 2)
    pieces = []
    for h in range(acc.shape[1] // LANES):
        x = acc[:, h * LANES:(h + 1) * LANES]
        sq = x * x
        s_lo = jnp.sum(jnp.where(lo, sq, 0.0), axis=-1, keepdims=True)
        s_hi = jnp.sum(jnp.where(lo, 0.0, sq), axis=-1, keepdims=True)
        ms = jnp.where(lo, s_lo, s_hi) * (2.0 / LANES)
        y = x * lax.rsqrt(ms + NORM_EPS) * gain
        pieces.append(y if scale == 1.0 else y * scale)
    outs[0][...] = jnp.concatenate(pieces, axis=1).astype(outs[0].dtype)


def _epi_plain2d(acc, extras, outs):
    outs[0][...] = acc.astype(outs[0].dtype)


def _epi_sigmoid(acc, extras, outs):
    outs[0][...] = jax.nn.sigmoid(acc).astype(outs[0].dtype)


def _epi_relu2(acc, extras, outs):
    r = jnp.maximum(acc, 0.0)
    outs[0][...] = (r * r).astype(outs[0].dtype)


def _epi_residual(acc, extras, outs):
    x_ref, gate_ref = extras
    outs[0][...] = x_ref[...] + gate_ref[...] * acc


def _lf_kernel(w_ref, h_ref, b_ref, o_ref):
    u = lax.dot_general(w_ref[...], h_ref[...], NT_DIMS, preferred_element_type=F32)
    o_ref[...] = _log_sigmoid(u + b_ref[...])


def _lf_proj(w_fa_t, h, b_fa):
    m, d_model = h.shape
    tm = _pick(m, 512)
    return pl.pallas_call(
        _lf_kernel,
        grid=(m // tm,),
        in_specs=[pl.BlockSpec((HEAD_PAD, d_model), lambda i: (0, 0)),
                  pl.BlockSpec((tm, d_model), lambda i: (i, 0)),
                  pl.BlockSpec((HEAD_PAD, 1), lambda i: (0, 0))],
        out_specs=pl.BlockSpec((HEAD_PAD, tm), lambda i: (0, i)),
        out_shape=jax.ShapeDtypeStruct((HEAD_PAD, m), F32),
        compiler_params=_cparams(1, 32),
    )(w_fa_t, h, b_fa)


def _ecum_kernel(x_ref, o_ref, loc_sc, tot_sc, *, n_chunks):
    x = x_ref[...].reshape(n_chunks * HEAD_PAD, LANES)
    u = _suffix_ones(LANES)
    hi, mid, lo = _split3(x)
    loc = (jnp.dot(hi, u, preferred_element_type=F32) + jnp.dot(mid, u, preferred_element_type=F32)
           + jnp.dot(lo, u, preferred_element_type=F32))
    loc_sc[...] = loc.reshape(n_chunks, HEAD_PAD, LANES)
    tot_sc[...] = jnp.sum(x, axis=-1, keepdims=True).reshape(n_chunks, HEAD_PAD, 1)

    def body(i, later):
        j = n_chunks - 1 - i
        o_ref[j] = loc_sc[j] + later
        return later + tot_sc[j]

    lax.fori_loop(0, n_chunks, body, jnp.zeros((HEAD_PAD, 1), F32))


def _ecum(x):
    nb, n_chunks = x.shape[0], x.shape[1]
    return pl.pallas_call(
        functools.partial(_ecum_kernel, n_chunks=n_chunks),
        grid=(nb,),
        in_specs=[pl.BlockSpec((None, n_chunks, HEAD_PAD, LANES), lambda b: (b, 0, 0, 0))],
        out_specs=pl.BlockSpec((None, n_chunks, HEAD_PAD, LANES), lambda b: (b, 0, 0, 0)),
        out_shape=jax.ShapeDtypeStruct(x.shape, F32),
        scratch_shapes=[pltpu.VMEM((n_chunks, HEAD_PAD, LANES), F32), pltpu.VMEM((n_chunks, HEAD_PAD, 1), F32)],
        compiler_params=_cparams(1, 32),
    )(x)


def _lf_gather_kernel(pt_ref, tab_ref, new_ref, o_ref, *, n_pages, n_heads):
    b = pl.program_id(0)
    sub = lax.broadcasted_iota(jnp.int32, (SUBLANES, LANES), 0)

    def body(j, carry):
        page = pt_ref[b, j]
        blk = lax.shift_right_logical(page, 3)
        row = jnp.bitwise_and(page, SUBLANES - 1)
        for h in range(n_heads):
            tile = tab_ref[h, blk]
            o_ref[j, h:h + 1, :] = jnp.sum(jnp.where(sub == row, tile, 0.0), axis=0, keepdims=True)
        o_ref[j, n_heads:HEAD_PAD, :] = jnp.zeros((HEAD_PAD - n_heads, LANES), F32)
        return carry

    lax.fori_loop(0, n_pages, body, 0)
    o_ref[n_pages] = new_ref[...]


def _lf_gather(page_table, table, lf_new):
    nb, n_pages = page_table.shape
    n_heads = table.shape[0]
    grid_spec = pltpu.PrefetchScalarGridSpec(
        num_scalar_prefetch=1,
        grid=(nb,),
        in_specs=[pl.BlockSpec(table.shape, lambda b, pt: (0, 0, 0, 0)),
                  pl.BlockSpec((None, HEAD_PAD, LANES), lambda b, pt: (b, 0, 0))],
        out_specs=pl.BlockSpec((None, n_pages + 1, HEAD_PAD, LANES), lambda b, pt: (b, 0, 0, 0)))
    return pl.pallas_call(
        functools.partial(_lf_gather_kernel, n_pages=n_pages, n_heads=n_heads),
        grid_spec=grid_spec,
        out_shape=jax.ShapeDtypeStruct((nb, n_pages + 1, HEAD_PAD, LANES), F32),
        compiler_params=_cparams(1, 40),
    )(page_table, table, lf_new)


def _softmax_step(s, v, carry):
    m, l, acc = carry
    m_new = jnp.maximum(m, jnp.max(s, axis=-1, keepdims=True))
    alpha = jnp.exp(m - m_new)
    p = jnp.exp(s - m_new)
    l = alpha * l + jnp.sum(p, axis=-1, keepdims=True)
    acc = alpha * acc + jnp.dot(p.astype(BF16), v, preferred_element_type=F32)
    return m_new, l, acc


def _causal_blocks(qi, tq, tk):
    return (qi * tq) // tk, -(-((qi + 1) * tq) // tk)


def _fox_prompt_kernel(q_ref, k_ref, v_ref, e_ref, o_ref, *, seq, tq, tk):
    for qi in range(seq // tq):
        q = q_ref[pl.ds(qi * tq, tq), :]

        def step(ki, carry, masked, q=q, qi=qi):
            k0 = pl.multiple_of(ki * tk, tk)
            k = k_ref[pl.ds(k0, tk), :].astype(BF16)
            v = v_ref[pl.ds(k0, tk), :].astype(BF16)
            s = lax.dot_general(q, k, NT_DIMS, preferred_element_type=F32) + e_ref[:, pl.ds(k0, tk)]
            if masked:
                row = lax.broadcasted_iota(jnp.int32, (tq, tk), 0) + qi * tq
                col = lax.broadcasted_iota(jnp.int32, (tq, tk), 1) + k0
                s = jnp.where(col <= row, s, NEG_INF)
            return _softmax_step(s, v, carry)

        n_full, n_end = _causal_blocks(qi, tq, tk)
        carry = (jnp.full((tq, 1), NEG_INF, F32), jnp.zeros((tq, 1), F32), jnp.zeros((tq, LANES), F32))
        for kd in range(n_full, n_end):
            carry = step(kd, carry, True)
        carry = lax.fori_loop(0, n_full, functools.partial(step, masked=False), carry)
        _, l, acc = carry
        o_ref[pl.ds(qi * tq, tq), :] = (acc / l).astype(o_ref.dtype)


def _fox_prompt(q, k, v, e_rows, tq, tk):
    nb, nh, seq, _ = q.shape
    hm = lambda b, h: (b, h, 0, 0)
    return pl.pallas_call(
        functools.partial(_fox_prompt_kernel, seq=seq, tq=tq, tk=tk),
        grid=(nb, nh),
        in_specs=[pl.BlockSpec((None, None, seq, LANES), hm),
                  pl.BlockSpec((None, None, seq, LANES), hm),
                  pl.BlockSpec((None, None, seq, LANES), hm),
                  pl.BlockSpec((None, None, 1, seq), hm)],
        out_specs=pl.BlockSpec((seq, LANES), lambda b, h: (b, h)),
        out_shape=jax.ShapeDtypeStruct((nb * seq, nh * LANES), BF16),
        compiler_params=_cparams(2, 32),
    )(q, k, v, e_rows)


def _diff_lambda(lq1, lk1, lq2, lk2, lam_init):
    return (jnp.exp(jnp.sum(lq1[...] * lk1[...], axis=-1, keepdims=True))
            - jnp.exp(jnp.sum(lq2[...] * lk2[...], axis=-1, keepdims=True)) + lam_init)


def _diff_finish(acc0, l0, acc1, l1, lam, gain, lam_init):
    o = acc0 / l0 - lam * (acc1 / l1)
    o = o * lax.rsqrt(jnp.mean(o * o, axis=-1, keepdims=True) + NORM_EPS) * gain
    return o * (1.0 - lam_init)


def _diff_prompt_kernel(q_ref, k_ref, v_ref, lq1, lk1, lq2, lk2, gs_ref, o_ref, *, seq, tq, tk, lam_init, n_heads):
    h = pl.program_id(1)
    slope = jnp.exp2(-(8.0 / n_heads) * (h + 1).astype(F32) * jnp.ones((1, tk), F32))
    lam = _diff_lambda(lq1, lk1, lq2, lk2, lam_init)
    lo = lax.broadcasted_iota(jnp.int32, (1, LANES), 1) < (LANES // 2)
    for qi in range(seq // tq):
        q = q_ref[pl.ds(qi * tq, tq), :]
        zero = jnp.zeros_like(q)
        q2 = jnp.concatenate([jnp.where(lo, q, zero), jnp.where(lo, zero, q)], axis=0)

        def step(ki, carry, masked, q2=q2, qi=qi):
            k0 = pl.multiple_of(ki * tk, tk)
            k = k_ref[pl.ds(k0, tk), :].astype(BF16)
            v = v_ref[pl.ds(k0, tk), :].astype(BF16)
            kpos = (lax.broadcasted_iota(jnp.int32, (1, tk), 1) + (k0 - qi * tq)).astype(F32)
            s = lax.dot_general(q2, k, NT_DIMS, preferred_element_type=F32) + slope * kpos
            if masked:
                row = lax.broadcasted_iota(jnp.int32, (2 * tq, tk), 0)
                row = jnp.where(row >= tq, row - tq, row) + qi * tq
                col = lax.broadcasted_iota(jnp.int32, (2 * tq, tk), 1) + k0
                s = jnp.where(col <= row, s, NEG_INF)
            return _softmax_step(s, v, carry)

        n_full, n_end = _causal_blocks(qi, tq, tk)
        carry = (jnp.full((2 * tq, 1), NEG_INF, F32), jnp.zeros((2 * tq, 1), F32), jnp.zeros((2 * tq, LANES), F32))
        for kd in range(n_full, n_end):
            carry = step(kd, carry, True)
        carry = lax.fori_loop(0, n_full, functools.partial(step, masked=False), carry)
        _, l, acc = carry
        out = _diff_finish(acc[:tq], l[:tq], acc[tq:], l[tq:], lam, gs_ref[...], lam_init)
        o_ref[pl.ds(qi * tq, tq), :] = out.astype(o_ref.dtype)


def _diff_prompt(q, k, v, lams, g_sub, nb, seq, tq, tk, lam_init):
    nh = q.shape[1] // LANES
    blk = pl.BlockSpec((seq, LANES), lambda b, h: (b, h))
    vec64 = pl.BlockSpec((1, LANES // 2), lambda b, h: (0, 0))
    return pl.pallas_call(
        functools.partial(_diff_prompt_kernel, seq=seq, tq=tq, tk=tk, lam_init=lam_init, n_heads=nh),
        grid=(nb, nh),
        in_specs=[blk, blk, blk, vec64, vec64, vec64, vec64, pl.BlockSpec((1, LANES), lambda b, h: (0, 0))],
        out_specs=blk,
        out_shape=jax.ShapeDtypeStruct(q.shape, BF16),
        compiler_params=_cparams(2, 32),
    )(q, k, v, *lams, g_sub)


def _sb_block(z, v, later, acc, u, valid):
    l1mb = -jnp.maximum(z, 0.0) - jnp.log(1.0 + jnp.exp(-jnp.abs(z)))
    if valid is not None:
        l1mb = jnp.where(valid, l1mb, 0.0)
    hi, mid, lo = _split3(l1mb)
    after = (jnp.dot(hi, u, preferred_element_type=F32) + jnp.dot(mid, u, preferred_element_type=F32)
             + jnp.dot(lo, u, preferred_element_type=F32)) + later
    a = jnp.exp(l1mb + z + after)
    if valid is not None:
        a = jnp.where(valid, a, 0.0)
    acc = acc + jnp.dot(a.astype(BF16), v, preferred_element_type=F32)
    return later + jnp.sum(l1mb, axis=-1, keepdims=True), acc


def _sb_prompt_kernel(q_ref, k_ref, v_ref, o_ref, *, seq, tq, tk):
    u = _suffix_ones(tk)
    for qi in range(seq // tq):
        q = q_ref[pl.ds(qi * tq, tq), :]

        def step(ki, carry, masked, q=q, qi=qi):
            k0 = pl.multiple_of(ki * tk, tk)
            k = k_ref[pl.ds(k0, tk), :].astype(BF16)
            v = v_ref[pl.ds(k0, tk), :].astype(BF16)
            z = lax.dot_general(q, k, NT_DIMS, preferred_element_type=F32)
            valid = None
            if masked:
                row = lax.broadcasted_iota(jnp.int32, (tq, tk), 0) + qi * tq
                col = lax.broadcasted_iota(jnp.int32, (tq, tk), 1) + k0
                valid = col < row
            return _sb_block(z, v, carry[0], carry[1], u, valid)

        n_full, n_end = _causal_blocks(qi, tq, tk)
        carry = (jnp.zeros((tq, 1), F32), jnp.zeros((tq, LANES), F32))
        for kd in reversed(range(n_full, n_end)):
            carry = step(kd, carry, True)
        carry = lax.fori_loop(0, n_full, lambda i, c: step(n_full - 1 - i, c, False), carry)
        o_ref[pl.ds(qi * tq, tq), :] = carry[1].astype(o_ref.dtype)


def _sb_prompt(q, k, v, tq, tk):
    nb, nh, seq, _ = q.shape
    hm = lambda b, h: (b, h, 0, 0)
    return pl.pallas_call(
        functools.partial(_sb_prompt_kernel, seq=seq, tq=tq, tk=tk),
        grid=(nb, nh),
        in_specs=[pl.BlockSpec((None, None, seq, LANES), hm)] * 3,
        out_specs=pl.BlockSpec((seq, LANES), lambda b, h: (b, h)),
        out_shape=jax.ShapeDtypeStruct((nb * seq, nh * LANES), BF16),
        compiler_params=_cparams(2, 32),
    )(q, k, v)


def _page_specs(cache_block, layer, n_per_step, page_of):
    def spec(p):
        return pl.BlockSpec((None, None) + cache_block,
                            lambda b, j, pt: (layer, pt[b, page_of(j, p)]) + (0,) * len(cache_block))
    return [spec(p) for p in range(n_per_step)]


def _new_token_mask(rows):
    t = jnp.bitwise_and(lax.broadcasted_iota(jnp.int32, (rows, LANES), 0), SUBLANES - 1)
    return lax.broadcasted_iota(jnp.int32, (rows, LANES), 1), t


def _fox_dec_kernel(pt_ref, q_ref, e_ref, en_ref, kn_ref, vn_ref, *rest, n_slots, n_heads, n_steps):
    k_refs, v_refs = rest[:n_slots], rest[n_slots:2 * n_slots]
    o_ref, m_sc, l_sc, acc_sc = rest[2 * n_slots:]
    j = pl.program_id(1)
    rows = n_heads * SUBLANES

    @pl.when(j == 0)
    def _():
        m_sc[...] = jnp.full((rows, 1), NEG_INF, F32)
        l_sc[...] = jnp.zeros((rows, 1), F32)
        acc_sc[...] = jnp.zeros((rows, LANES), F32)

    def scores(k_of, e_of):
        return jnp.concatenate(
            [lax.dot_general(q_ref[h], k_of(h).astype(BF16), NT_DIMS, preferred_element_type=F32) + e_of(h)
             for h in range(n_heads)], axis=0)

    def update(s, v_of):
        m_prev = m_sc[...]
        m_new = jnp.maximum(m_prev, jnp.max(s, axis=-1, keepdims=True))
        alpha = jnp.exp(m_prev - m_new)
        p = jnp.exp(s - m_new)
        l_sc[...] = alpha * l_sc[...] + jnp.sum(p, axis=-1, keepdims=True)
        pb = p.astype(BF16)
        pv = []
        for h in range(n_heads):
            tot = None
            for pg in range(s.shape[1] // LANES):
                part = jnp.dot(pb[h * SUBLANES:(h + 1) * SUBLANES, pg * LANES:(pg + 1) * LANES],
                               v_of(pg, h).astype(BF16), preferred_element_type=F32)
                tot = part if tot is None else tot + part
            pv.append(tot)
        acc_sc[...] = alpha * acc_sc[...] + jnp.concatenate(pv, axis=0)
        m_sc[...] = m_new

    s_past = jnp.concatenate(
        [scores(lambda h, p=p: k_refs[p][h], lambda h, p=p: e_ref[p, h:h + 1, :]) for p in range(n_slots)], axis=1)
    update(s_past, lambda pg, h: v_refs[pg][h])

    @pl.when(j == n_steps - 1)
    def _():
        col, t = _new_token_mask(rows)
        s_new = scores(lambda h: kn_ref[h], lambda h: en_ref[h:h + 1, :])
        update(jnp.where(col <= t, s_new, NEG_INF), lambda pg, h: vn_ref[h])
        o_ref[...] = (acc_sc[...] / l_sc[...]).reshape(n_heads, SUBLANES, LANES).astype(o_ref.dtype)


def _fox_decode(page_table, q, e_all, k_new, v_new, cache_k, cache_v, layer, n_slots):
    nb, n_pages = page_table.shape
    n_heads = q.shape[1]
    n_steps = n_pages // n_slots
    page_blk = (n_heads, LANES, LANES)
    page_of = lambda j, p: j * n_slots + p
    per_b = lambda b, j, pt: (b, 0, 0, 0)
    grid_spec = pltpu.PrefetchScalarGridSpec(
        num_scalar_prefetch=1,
        grid=(nb, n_steps),
        in_specs=[pl.BlockSpec((None, n_heads, SUBLANES, LANES), per_b),
                  pl.BlockSpec((None, n_slots, HEAD_PAD, LANES), lambda b, j, pt: (b, j, 0, 0)),
                  pl.BlockSpec((None, None, HEAD_PAD, LANES), lambda b, j, pt: (b, n_pages, 0, 0)),
                  pl.BlockSpec((None,) + page_blk, per_b),
                  pl.BlockSpec((None,) + page_blk, per_b)]
                 + _page_specs(page_blk, layer, n_slots, page_of) + _page_specs(page_blk, layer, n_slots, page_of),
        out_specs=pl.BlockSpec((None, n_heads, SUBLANES, LANES), per_b),
        scratch_shapes=[pltpu.VMEM((n_heads * SUBLANES, 1), F32), pltpu.VMEM((n_heads * SUBLANES, 1), F32),
                        pltpu.VMEM((n_heads * SUBLANES, LANES), F32)])
    return pl.pallas_call(
        functools.partial(_fox_dec_kernel, n_slots=n_slots, n_heads=n_heads, n_steps=n_steps),
        grid_spec=grid_spec,
        out_shape=jax.ShapeDtypeStruct((nb, n_heads, SUBLANES, LANES), BF16),
        compiler_params=_cparams(2, 48),
    )(page_table, q, e_all, e_all, k_new, v_new, *([cache_k] * n_slots), *([cache_v] * n_slots))


def _diff_dec_kernel(pt_ref, q_ref, kn_ref, vn_ref, lq1, lk1, lq2, lk2, gs_ref, *rest,
                     n_slots, n_heads, n_steps, past, lam_init):
    k_refs, v_refs = rest[:n_slots], rest[n_slots:2 * n_slots]
    o_ref, m_sc, l_sc, acc_sc = rest[2 * n_slots:]
    j = pl.program_id(1)
    hr = 2 * SUBLANES
    rows = n_heads * hr
    lo = lax.broadcasted_iota(jnp.int32, (1, LANES), 1) < (LANES // 2)

    @pl.when(j == 0)
    def _():
        m_sc[...] = jnp.full((rows, 1), NEG_INF, F32)
        l_sc[...] = jnp.zeros((rows, 1), F32)
        acc_sc[...] = jnp.zeros((rows, LANES), F32)

    col, t = _new_token_mask(hr)

    def scores(kt_of, first_pos):
        dist = ((past + t) - (first_pos + col)).astype(F32)
        out = []
        for h in range(n_heads):
            q = q_ref[h]
            zero = jnp.zeros_like(q)
            q2 = jnp.concatenate([jnp.where(lo, q, zero), jnp.where(lo, zero, q)], axis=0)
            s = jnp.dot(q2, kt_of(h).astype(BF16), preferred_element_type=F32)
            out.append(s - (2.0 ** (-(8.0 / n_heads) * (h + 1))) * dist)
        return jnp.concatenate(out, axis=0)

    def update(s, v_of):
        m_prev = m_sc[...]
        m_new = jnp.maximum(m_prev, jnp.max(s, axis=-1, keepdims=True))
        alpha = jnp.exp(m_prev - m_new)
        p = jnp.exp(s - m_new)
        l_sc[...] = alpha * l_sc[...] + jnp.sum(p, axis=-1, keepdims=True)
        pb = p.astype(BF16)
        pv = []
        for h in range(n_heads):
            tot = None
            for pg in range(s.shape[1] // LANES):
                part = jnp.dot(pb[h * hr:(h + 1) * hr, pg * LANES:(pg + 1) * LANES],
                               v_of(pg, h).astype(BF16), preferred_element_type=F32)
                tot = part if tot is None else tot + part
            pv.append(tot)
        acc_sc[...] = alpha * acc_sc[...] + jnp.concatenate(pv, axis=0)
        m_sc[...] = m_new

    s_past = jnp.concatenate(
        [scores(lambda h, p=p: k_refs[p][h], (j * n_slots + p) * LANES) for p in range(n_slots)], axis=1)
    update(s_past, lambda pg, h: v_refs[pg][:, h, :])

    @pl.when(j == n_steps - 1)
    def _():
        s_new = scores(lambda h: kn_ref[h], past)
        valid = jnp.concatenate([col <= t] * n_heads, axis=0)
        update(jnp.where(valid, s_new, NEG_INF), lambda pg, h: vn_ref[:, h, :])
        lam = _diff_lambda(lq1, lk1, lq2, lk2, lam_init)
        acc, l = acc_sc[...], l_sc[...]
        for h in range(n_heads):
            r0 = h * hr
            out = _diff_finish(acc[r0:r0 + SUBLANES], l[r0:r0 + SUBLANES], acc[r0 + SUBLANES:r0 + hr],
                               l[r0 + SUBLANES:r0 + hr], lam, gs_ref[...], lam_init)
            o_ref[h] = out.astype(o_ref.dtype)


def _diff_decode(page_table, q, kt_new, v_new, lams, g_sub, cache_kt, cache_v, layer, n_slots, lam_init):
    nb, n_pages = page_table.shape
    n_heads = q.shape[1]
    n_steps = n_pages // n_slots
    kt_blk, v_blk = (n_heads, LANES, LANES), (LANES, n_heads, LANES)
    page_of = lambda j, p: j * n_slots + p
    per_b = lambda b, j, pt: (b, 0, 0, 0)
    vec64 = pl.BlockSpec((1, LANES // 2), lambda b, j, pt: (0, 0))
    rows = n_heads * 2 * SUBLANES
    grid_spec = pltpu.PrefetchScalarGridSpec(
        num_scalar_prefetch=1,
        grid=(nb, n_steps),
        in_specs=[pl.BlockSpec((None, n_heads, SUBLANES, LANES), per_b),
                  pl.BlockSpec((None,) + kt_blk, per_b),
                  pl.BlockSpec((None,) + v_blk, per_b),
                  vec64, vec64, vec64, vec64, pl.BlockSpec((1, LANES), lambda b, j, pt: (0, 0))]
                 + _page_specs(kt_blk, layer, n_slots, page_of) + _page_specs(v_blk, layer, n_slots, page_of),
        out_specs=pl.BlockSpec((None, n_heads, SUBLANES, LANES), per_b),
        scratch_shapes=[pltpu.VMEM((rows, 1), F32), pltpu.VMEM((rows, 1), F32), pltpu.VMEM((rows, LANES), F32)])
    return pl.pallas_call(
        functools.partial(_diff_dec_kernel, n_slots=n_slots, n_heads=n_heads, n_steps=n_steps,
                          past=n_pages * LANES, lam_init=lam_init),
        grid_spec=grid_spec,
        out_shape=jax.ShapeDtypeStruct((nb, n_heads, SUBLANES, LANES), BF16),
        compiler_params=_cparams(2, 48),
    )(page_table, q, kt_new, v_new, *lams, g_sub, *([cache_kt] * n_slots), *([cache_v] * n_slots))


def _sb_dec_kernel(pt_ref, q_ref, kn_ref, vn_ref, *rest, n_slots, n_heads, n_steps):
    k_refs, v_refs = rest[:n_slots], rest[n_slots:2 * n_slots]
    o_ref, later_sc, acc_sc = rest[2 * n_slots:]
    j = pl.program_id(1)
    rows = n_heads * SUBLANES
    u = _suffix_ones(LANES)

    def page(k_of, v_of, valid):
        z = jnp.concatenate(
            [lax.dot_general(q_ref[h], k_of(h).astype(BF16), NT_DIMS, preferred_element_type=F32)
             for h in range(n_heads)], axis=0)
        l1mb = -jnp.maximum(z, 0.0) - jnp.log(1.0 + jnp.exp(-jnp.abs(z)))
        if valid is not None:
            l1mb = jnp.where(valid, l1mb, 0.0)
        hi, mid, lo = _split3(l1mb)
        after = (jnp.dot(hi, u, preferred_element_type=F32) + jnp.dot(mid, u, preferred_element_type=F32)
                 + jnp.dot(lo, u, preferred_element_type=F32)) + later_sc[...]
        a = jnp.exp(l1mb + z + after)
        if valid is not None:
            a = jnp.where(valid, a, 0.0)
        ab = a.astype(BF16)
        pv = [jnp.dot(ab[h * SUBLANES:(h + 1) * SUBLANES], v_of(h).astype(BF16), preferred_element_type=F32)
              for h in range(n_heads)]
        acc_sc[...] += jnp.concatenate(pv, axis=0)
        later_sc[...] += jnp.sum(l1mb, axis=-1, keepdims=True)

    @pl.when(j == 0)
    def _():
        later_sc[...] = jnp.zeros((rows, 1), F32)
        acc_sc[...] = jnp.zeros((rows, LANES), F32)
        col, t = _new_token_mask(rows)
        page(lambda h: kn_ref[h], lambda h: vn_ref[h], col < t)

    for p in range(n_slots):
        page(lambda h, p=p: k_refs[p][h], lambda h, p=p: v_refs[p][h], None)

    @pl.when(j == n_steps - 1)
    def _():
        o_ref[...] = acc_sc[...].reshape(n_heads, SUBLANES, LANES).astype(o_ref.dtype)


def _sb_decode(page_table, q, k_new, v_new, cache_k, cache_v, layer, n_slots):
    nb, n_pages = page_table.shape
    n_heads = q.shape[1]
    n_steps = n_pages // n_slots
    page_blk = (n_heads, LANES, LANES)
    page_of = lambda j, p: n_pages - 1 - (j * n_slots + p)
    per_b = lambda b, j, pt: (b, 0, 0, 0)
    grid_spec = pltpu.PrefetchScalarGridSpec(
        num_scalar_prefetch=1,
        grid=(nb, n_steps),
        in_specs=[pl.BlockSpec((None, n_heads, SUBLANES, LANES), per_b),
                  pl.BlockSpec((None,) + page_blk, per_b),
                  pl.BlockSpec((None,) + page_blk, per_b)]
                 + _page_specs(page_blk, layer, n_slots, page_of) + _page_specs(page_blk, layer, n_slots, page_of),
        out_specs=pl.BlockSpec((None, n_heads, SUBLANES, LANES), per_b),
        scratch_shapes=[pltpu.VMEM((n_heads * SUBLANES, 1), F32), pltpu.VMEM((n_heads * SUBLANES, LANES), F32)])
    return pl.pallas_call(
        functools.partial(_sb_dec_kernel, n_slots=n_slots, n_heads=n_heads, n_steps=n_steps),
        grid_spec=grid_spec,
        out_shape=jax.ShapeDtypeStruct((nb, n_heads, SUBLANES, LANES), BF16),
        compiler_params=_cparams(2, 48),
    )(page_table, q, k_new, v_new, *([cache_k] * n_slots), *([cache_v] * n_slots))


def _merge_kernel(oa, ob, oc, wa, wb, wc, g0, g1, g2, o_ref):
    pa = jnp.dot(oa[...], wa[...].astype(BF16), preferred_element_type=F32)
    pb = jnp.dot(ob[...], wb[...].astype(BF16), preferred_element_type=F32)
    pc = jnp.dot(oc[...], wc[...].astype(BF16), preferred_element_type=F32)
    o_ref[...] = (g0[...] * pa + g1[...] * pb + g2[...] * pc).astype(o_ref.dtype)


def _merge(oa, ob, oc, wa, wb, wc, gates, tm, tn):
    m = oa.shape[0]
    d_model = wa.shape[1]
    nj = d_model // tn
    a_spec = lambda a: pl.BlockSpec((tm, a.shape[1]), lambda i, j: (i, 0))
    w_spec = lambda w: pl.BlockSpec((w.shape[0], tn), lambda i, j: (0, j))
    g_spec = lambda c: pl.BlockSpec((tm, tn), lambda i, j: (i, c * nj + j))
    return pl.pallas_call(
        _merge_kernel,
        grid=(m // tm, nj),
        in_specs=[a_spec(oa), a_spec(ob), a_spec(oc), w_spec(wa), w_spec(wb), w_spec(wc),
                  g_spec(0), g_spec(1), g_spec(2)],
        out_specs=pl.BlockSpec((tm, tn), lambda i, j: (i, j)),
        out_shape=jax.ShapeDtypeStruct((m, d_model), BF16),
        compiler_params=_cparams(2, 48),
    )(oa, ob, oc, wa, wb, wc, gates, gates, gates)


def _project(h, w, g, col0, ncols, tn, epilogue, extras=(), extra_specs=(), heads=None, dtype=F32):
    m, d_model = h.shape
    if heads is not None:
        shape, spec = _heads_out(g, heads, tn // LANES, dtype)
    else:
        shape = jax.ShapeDtypeStruct((m, ncols), dtype)
        spec = pl.BlockSpec((g.tm, tn), lambda i, j, k: (i, j))
    return _mm(h, w, col0, ncols, g.tm, tn, d_model, epilogue, list(extras), list(extra_specs), shape, spec)


def _vec_spec(n):
    return pl.BlockSpec((1, n), lambda i, j, k: (0, 0))


def _layer_front(x, mod, g, p, dims):
    fox_w, dqk_w, dv_w, sb_w, d_model = dims
    fh, sh = fox_w // LANES, sb_w // LANES
    h = _prenorm(x, p["g_norm1"], mod, g, 1, 0)
    w = p["w_in"]
    tn = 512
    sc128 = 1.0 / math.sqrt(LANES)
    rms = lambda scale: functools.partial(_epi_heads_rms, g=g, scale=scale)
    plain = lambda scale: functools.partial(_epi_heads_plain, g=g, scale=scale)
    half = lambda scale: functools.partial(_epi_rms_half, scale=scale)
    c = 0
    out = {}
    proj = functools.partial(_project, h, w, g)
    out["qa"] = proj(c, fox_w, tn, rms(sc128), [p["g_q_fox"]], [_vec_spec(LANES)], heads=fh, dtype=BF16); c += fox_w
    out["ka"] = proj(c, fox_w, tn, rms(1.0), [p["g_k_fox"]], [_vec_spec(LANES)], heads=fh); c += fox_w
    out["va"] = proj(c, fox_w, tn, plain(1.0), heads=fh); c += fox_w
    out["qb"] = proj(c, dqk_w, tn, half(0.125), [p["g_q_diff"]], [_vec_spec(LANES)], dtype=BF16); c += dqk_w
    out["kb"] = proj(c, dqk_w, tn, half(1.0), [p["g_k_diff"]], [_vec_spec(LANES)]); c += dqk_w
    out["vb"] = proj(c, dv_w, tn, _epi_plain2d); c += dv_w
    out["qc"] = proj(c, sb_w, tn, plain(sc128), heads=sh, dtype=BF16); c += sb_w
    out["kc"] = proj(c, sb_w, tn, plain(1.0), heads=sh); c += sb_w
    out["vc"] = proj(c, sb_w, tn, plain(1.0), heads=sh); c += sb_w
    out["gates"] = proj(c, 3 * d_model, tn, _epi_sigmoid, dtype=BF16)
    out["lf_t"] = _lf_proj(p["w_fa_t"], h, p["b_fa"])
    return out


def _layer_back(x, mod, g, p, oa, ob, oc, gates):
    m, d_model = x.shape
    merged = _merge(oa, ob, oc, p["w_br_fox"], p["w_br_diff"], p["w_br_sb"], gates, g.tm, 256)
    tn = 512
    xspec = pl.BlockSpec((g.tm, tn), lambda i, j, k: (i, j))
    oshape = jax.ShapeDtypeStruct((m, d_model), F32)
    x1 = _mm(merged, p["w_out"], 0, d_model, g.tm, tn, d_model, _epi_residual,
             [x, mod], [xspec, _mod_spec(g, 2, d_model, tn)], oshape, xspec)
    h2 = _prenorm(x1, p["g_norm2"], mod, g, 4, 3)
    d_ff = p["w_up"].shape[1]
    f = _mm(h2, p["w_up"], 0, d_ff, g.tm, tn, d_model, _epi_relu2, [], [],
            jax.ShapeDtypeStruct((m, d_ff), BF16), xspec)
    tn2 = _pick(d_model, 1024)
    xspec2 = pl.BlockSpec((g.tm, tn2), lambda i, j, k: (i, j))
    return _mm(f, p["w_down"], 0, d_model, g.tm, tn2, _pick(d_ff, 1024), _epi_residual,
               [x1, mod], [xspec2, _mod_spec(g, 5, d_model, tn2)], oshape, xspec2)


def kernel(x_prompt, x_sample, cache_fox_k, cache_fox_v, cache_fox_logf, cache_diff_k, cache_diff_v, cache_sb_k, cache_sb_v, page_table, c_prompt, c_sample, w_ada, b_ada, g_norm1, w_in, b_fgate, g_q_fox, g_k_fox, g_q_diff, g_k_diff, lambda_q1, lambda_k1, lambda_q2, lambda_k2, g_sub_diff, w_br_fox, w_br_diff, w_br_sb, w_out, g_norm2, w_up, w_down):
    n_layers = w_ada.shape[0]
    nbp, seq, d_model = x_prompt.shape
    nbs, dseq, _ = x_sample.shape
    assert dseq == SUBLANES, "decode kernels hold the new tokens of a sequence in one sublane group"
    n_pages = page_table.shape[1]
    n_pool, page_size, fh, fd = cache_fox_k.shape[1:]
    dh, dqk = cache_diff_k.shape[3], cache_diff_k.shape[5]
    sh = cache_sb_k.shape[3]
    assert page_size == LANES and fd == LANES and 2 * dqk == LANES and cache_sb_k.shape[4] == LANES
    fox_w, dqk_w, dv_w, sb_w = fh * LANES, dh * LANES, dh * LANES, sh * LANES
    dims = (fox_w, dqk_w, dv_w, sb_w, d_model)
    past = n_pages * page_size
    mp, ms = nbp * seq, nbs * dseq

    gp = Group(nbp, seq, _pick(seq, 1024), False)
    gs = Group(nbs, dseq, ms, True)
    tq = tk = _pick(seq, 512)
    n_slots = 4 if n_pages % 4 == 0 else 1

    n_c = nbp + nbs
    c_rows = -(-n_c // SUBLANES) * SUBLANES
    c_all = jnp.concatenate([c_prompt, c_sample, jnp.zeros((c_rows - n_c, d_model), F32)], axis=0)
    mod_all = _ada(c_all, w_ada, b_ada)

    fox_k_hm = jnp.transpose(cache_fox_k, (0, 1, 3, 2, 4))
    fox_v_hm = jnp.transpose(cache_fox_v, (0, 1, 3, 2, 4))
    sb_k_hm = jnp.transpose(cache_sb_k, (0, 1, 3, 2, 4))
    sb_v_hm = jnp.transpose(cache_sb_v, (0, 1, 3, 2, 4))
    diff_kt = jnp.transpose(cache_diff_k, (0, 1, 3, 4, 5, 2)).reshape(n_layers, n_pool, dh, LANES, page_size)
    logf_t = jnp.transpose(cache_fox_logf, (0, 3, 1, 2))
    pool_pad = -(-n_pool // SUBLANES) * SUBLANES

    c_fa = 3 * fox_w
    n_fa = fh
    yp, ys = x_prompt.reshape(mp, d_model), x_sample.reshape(ms, d_model)
    outs_p, outs_s = [], []
    for l in range(n_layers):
        lam_init = 0.8 - 0.6 * math.exp(-0.3 * l)
        w_l = w_in[l]
        p = {
            "w_in": jnp.concatenate([w_l[:, :c_fa], w_l[:, c_fa + n_fa:]], axis=1).astype(BF16),
            "w_fa_t": jnp.pad(w_l[:, c_fa:c_fa + n_fa].T, ((0, HEAD_PAD - n_fa), (0, 0))).astype(BF16),
            "b_fa": jnp.pad(b_fgate[l], (0, HEAD_PAD - n_fa)).reshape(HEAD_PAD, 1),
            "g_norm1": g_norm1[l], "g_norm2": g_norm2[l],
            "g_q_fox": g_q_fox[l].reshape(1, LANES), "g_k_fox": g_k_fox[l].reshape(1, LANES),
            "g_q_diff": jnp.tile(g_q_diff[l], 2).reshape(1, LANES), "g_k_diff": jnp.tile(g_k_diff[l], 2).reshape(1, LANES),
            "w_br_fox": w_br_fox[l], "w_br_diff": w_br_diff[l], "w_br_sb": w_br_sb[l],
            "w_out": w_out[l], "w_up": w_up[l], "w_down": w_down[l].astype(BF16),
        }
        lams = [v[l].reshape(1, dqk) for v in (lambda_q1, lambda_k1, lambda_q2, lambda_k2)]
        g_sub = g_sub_diff[l].reshape(1, LANES)

        mod_p = mod_all[l, :nbp].reshape(nbp, 1, -1)
        fr = _layer_front(yp, mod_p, gp, p, dims)
        lf_p = fr["lf_t"]
        x_e = lf_p.reshape(HEAD_PAD, nbp, seq // LANES, LANES).transpose(1, 2, 0, 3)
        e_p = _ecum(x_e).transpose(0, 2, 1, 3).reshape(nbp, HEAD_PAD, 1, seq)
        oa = _fox_prompt(fr["qa"], fr["ka"], fr["va"], e_p, tq, tk)
        ob = _diff_prompt(fr["qb"], fr["kb"], fr["vb"], lams, g_sub, nbp, seq, tq // 2, tk, lam_init)
        oc = _sb_prompt(fr["qc"], fr["kc"], fr["vc"], min(tq, 256), min(tk, 256))
        yp = _layer_back(yp, mod_p, gp, p, oa, ob, oc, fr["gates"])
        hm_out = lambda a: jnp.transpose(a, (0, 2, 1, 3))
        outs_p.append((
            hm_out(fr["ka"]), hm_out(fr["va"]), lf_p[:fh].reshape(fh, nbp, seq).transpose(1, 2, 0),
            fr["kb"].reshape(nbp, seq, dh, 2, dqk), fr["vb"].reshape(nbp, seq, dh, LANES),
            hm_out(fr["kc"]), hm_out(fr["vc"])))

        mod_s = jnp.repeat(mod_all[l, nbp:nbp + nbs], dseq, axis=0)
        fs = _layer_front(ys, mod_s, gs, p, dims)
        lf_s = fs["lf_t"]
        lf_new = jnp.pad(lf_s.reshape(HEAD_PAD, nbs, dseq).transpose(1, 0, 2), ((0, 0), (0, 0), (0, LANES - dseq)))
        table = jnp.pad(logf_t[l], ((0, 0), (0, pool_pad - n_pool), (0, 0))).reshape(fh, pool_pad // SUBLANES, SUBLANES, LANES)
        e_s = _ecum(_lf_gather(page_table, table, lf_new))
        pad_keys = lambda a: jnp.pad(a, ((0, 0), (0, 0), (0, LANES - dseq), (0, 0)))
        oa = _fox_decode(page_table, fs["qa"], e_s, pad_keys(fs["ka"]), pad_keys(fs["va"]), fox_k_hm, fox_v_hm, l, n_slots)
        qb_hm = fs["qb"].reshape(nbs, dseq, dh, LANES).transpose(0, 2, 1, 3)
        kb_t = fs["kb"].reshape(nbs, dseq, dh, LANES).transpose(0, 2, 3, 1)
        kt_new = jnp.pad(kb_t, ((0, 0), (0, 0), (0, 0), (0, LANES - dseq)))
        vb_new = jnp.pad(fs["vb"].reshape(nbs, dseq, dh, LANES), ((0, 0), (0, LANES - dseq), (0, 0), (0, 0)))
        ob = _diff_decode(page_table, qb_hm, kt_new, vb_new, lams, g_sub, diff_kt, cache_diff_v, l, n_slots, lam_init)
        oc = _sb_decode(page_table, fs["qc"], pad_keys(fs["kc"]), pad_keys(fs["vc"]), sb_k_hm, sb_v_hm, l, n_slots)
        rows2d = lambda a: jnp.transpose(a, (0, 2, 1, 3)).reshape(ms, -1)
        ys = _layer_back(ys, mod_s, gs, p, rows2d(oa), rows2d(ob), rows2d(oc), fs["gates"])
        outs_s.append((
            hm_out(fs["ka"]), hm_out(fs["va"]), lf_s[:fh].reshape(fh, nbs, dseq).transpose(1, 2, 0),
            fs["kb"].reshape(nbs, dseq, dh, 2, dqk), fs["vb"].reshape(nbs, dseq, dh, LANES),
            hm_out(fs["kc"]), hm_out(fs["vc"])))

    stack = lambda outs: tuple(jnp.stack([o[i] for o in outs]) for i in range(7))
    return (yp.reshape(nbp, seq, d_model), ys.reshape(nbs, dseq, d_model)) + stack(outs_p) + stack(outs_s)
```

```python
import functools
import math
from typing import NamedTuple

import jax
import jax.numpy as jnp
from jax import lax
from jax.experimental import pallas as pl
from jax.experimental.pallas import tpu as pltpu

F32 = jnp.float32
BF16 = jnp.bfloat16
NORM_EPS = 1e-6
LANES = 128
SUBLANES = 8
HEAD_PAD = 16
NT_DIMS = (((1,), (1,)), ((), ()))
MIB = 1 << 20
NEG_INF = float("-inf")


class Group(NamedTuple):
    B: int
    T: int
    tm: int
    per_row_mod: bool


def _pick(dim, pref):
    t = min(pref, dim)
    while t >= LANES:
        if dim % t == 0 and t % LANES == 0:
            return t
        t -= LANES
    return dim


def _cparams(n_axes, vmem_mib=None):
    return pltpu.CompilerParams(
        dimension_semantics=("arbitrary",) * n_axes,
        vmem_limit_bytes=None if vmem_mib is None else vmem_mib * MIB)


def _log_sigmoid(x):
    return jnp.minimum(x, 0.0) - jnp.log(1.0 + jnp.exp(-jnp.abs(x)))


def _split3(x):
    hi = x.astype(BF16)
    r1 = x - hi.astype(F32)
    mid = r1.astype(BF16)
    lo = (r1 - mid.astype(F32)).astype(BF16)
    return hi, mid, lo


def _suffix_sum(x, u):
    hi = x.astype(BF16)
    lo = (x - hi.astype(F32)).astype(BF16)
    return jnp.dot(hi, u, preferred_element_type=F32) + jnp.dot(lo, u, preferred_element_type=F32)


def _div_mod(x, n):
    if n & (n - 1) == 0:
        return lax.shift_right_logical(x, n.bit_length() - 1), jnp.bitwise_and(x, n - 1)
    return lax.div(x, n), lax.rem(x, n)


def _suffix_ones(n):
    r = lax.broadcasted_iota(jnp.int32, (n, n), 0)
    c = lax.broadcasted_iota(jnp.int32, (n, n), 1)
    return (r > c).astype(BF16)


def _mod_spec(g, chunk, d_model, tn):
    nj = d_model // tn
    if g.per_row_mod:
        return pl.BlockSpec((g.tm, tn), lambda i, j, *_: (i, chunk * nj + j))
    nt = g.T // g.tm
    return pl.BlockSpec((None, 1, tn), lambda i, j, *_: (i // nt, 0, chunk * nj + j))


def _ada_kernel(c_ref, w_ref, b_ref, o_ref):
    c = c_ref[...]
    a = (c * jax.nn.sigmoid(c)).astype(BF16)
    o_ref[...] = jnp.dot(a, w_ref[...].astype(BF16), preferred_element_type=F32) + b_ref[...]


def _ada(c_all, w_ada, b_ada):
    n_layers, d_model, n = w_ada.shape
    rows = c_all.shape[0]
    tn = _pick(n, 512)
    return pl.pallas_call(
        _ada_kernel,
        grid=(n_layers, n // tn),
        in_specs=[pl.BlockSpec((rows, d_model), lambda l, j: (0, 0)),
                  pl.BlockSpec((None, d_model, tn), lambda l, j: (l, 0, j)),
                  pl.BlockSpec((None, 1, tn), lambda l, j: (l, 0, j))],
        out_specs=pl.BlockSpec((None, rows, tn), lambda l, j: (l, 0, j)),
        out_shape=jax.ShapeDtypeStruct((n_layers, rows, n), F32),
        compiler_params=_cparams(2, 40),
        name="ada_modulation",
    )(c_all, w_ada, b_ada.reshape(n_layers, 1, n))


def _prenorm_kernel(x_ref, g_ref, sc_ref, sh_ref, o_ref):
    x = x_ref[...]
    ms = jnp.mean(x * x, axis=-1, keepdims=True)
    y = x * lax.rsqrt(ms + NORM_EPS) * g_ref[...]
    o_ref[...] = (y * (1.0 + sc_ref[...]) + sh_ref[...]).astype(o_ref.dtype)


def _prenorm(x, gain, mod, g, scale_chunk, shift_chunk):
    m, d_model = x.shape
    tm = min(g.tm, 256)
    gg = g._replace(tm=tm)
    return pl.pallas_call(
        _prenorm_kernel,
        grid=(m // tm, 1),
        in_specs=[pl.BlockSpec((tm, d_model), lambda i, j: (i, 0)),
                  pl.BlockSpec((1, d_model), lambda i, j: (0, 0)),
                  _mod_spec(gg, scale_chunk, d_model, d_model),
                  _mod_spec(gg, shift_chunk, d_model, d_model)],
        out_specs=pl.BlockSpec((tm, d_model), lambda i, j: (i, 0)),
        out_shape=jax.ShapeDtypeStruct((m, d_model), BF16),
        compiler_params=_cparams(2, 40),
        name="mod_rmsnorm",
    )(x, gain.reshape(1, d_model), mod, mod)


def _mm_kernel(*refs, nk, n_extra, epilogue, w_rows_are_outputs):
    a_ref, w_ref = refs[0], refs[1]
    extras = refs[2:2 + n_extra]
    if w_rows_are_outputs:
        part = lax.dot_general(a_ref[...], w_ref[...].astype(BF16), NT_DIMS, preferred_element_type=F32)
    else:
        part = jnp.dot(a_ref[...], w_ref[...].astype(BF16), preferred_element_type=F32)
    if nk == 1:
        epilogue(part, extras, refs[2 + n_extra:])
        return
    outs, acc_ref = refs[2 + n_extra:-1], refs[-1]
    k = pl.program_id(2)

    @pl.when(k == 0)
    def _():
        acc_ref[...] = part

    @pl.when(k > 0)
    def _():
        acc_ref[...] += part

    @pl.when(k == nk - 1)
    def _():
        epilogue(acc_ref[...], extras, outs)


def _mm(name, a, w, layer, col0, ncols, tm, tn, tk, epilogue, extras, extra_specs, out_shape, out_specs, vmem_mib=48):
    m, kdim = a.shape
    assert col0 % tn == 0 and ncols % tn == 0 and m % tm == 0 and kdim % tk == 0
    nk, c0 = kdim // tk, col0 // tn
    if layer is None:
        w_spec = pl.BlockSpec((tn, tk), lambda i, j, k: (c0 + j, k))
    else:
        w_spec = pl.BlockSpec((None, tk, tn), lambda i, j, k: (layer, k, c0 + j))
    kern = functools.partial(_mm_kernel, nk=nk, n_extra=len(extras), epilogue=epilogue,
                             w_rows_are_outputs=layer is None)
    return pl.pallas_call(
        kern,
        grid=(m // tm, ncols // tn, nk),
        in_specs=[pl.BlockSpec((tm, tk), lambda i, j, k: (i, k)), w_spec] + list(extra_specs),
        out_specs=out_specs,
        out_shape=out_shape,
        scratch_shapes=[pltpu.VMEM((tm, tn), F32)] if nk > 1 else [],
        compiler_params=_cparams(3, vmem_mib),
        name=name,
    )(a, w, *extras)


def _store_heads(val, o_ref, g):
    for h in range(val.shape[1] // LANES):
        piece = val[:, h * LANES:(h + 1) * LANES].astype(o_ref.dtype)
        if g.per_row_mod:
            o_ref[:, h] = piece.reshape(g.B, g.T, LANES)
        else:
            o_ref[h] = piece


def _heads_out(g, n_heads, nh_tile, dtype):
    shape = jax.ShapeDtypeStruct((g.B, n_heads, g.T, LANES), dtype)
    if g.per_row_mod:
        return shape, pl.BlockSpec((g.B, nh_tile, g.T, LANES), lambda i, j, k: (0, j, 0, 0))
    nt = g.T // g.tm
    return shape, pl.BlockSpec((None, nh_tile, g.tm, LANES), lambda i, j, k: (i // nt, j, i % nt, 0))


def _epi_heads_plain(acc, extras, outs, *, g, scale):
    _store_heads(acc if scale == 1.0 else acc * scale, outs[0], g)


def _epi_heads_rms(acc, extras, outs, *, g, scale):
    gain = extras[0][...]
    pieces = []
    for h in range(acc.shape[1] // LANES):
        x = acc[:, h * LANES:(h + 1) * LANES]
        y = x * lax.rsqrt(jnp.mean(x * x, axis=-1, keepdims=True) + NORM_EPS) * gain
        pieces.append(y if scale == 1.0 else y * scale)
    _store_heads(jnp.concatenate(pieces, axis=1), outs[0], g)


def _epi_rms_half(acc, extras, outs, *, scale):
    gain = extras[0][...]
    lo = lax.broadcasted_iota(jnp.int32, (1, LANES), 1) < (LANES // 2)
    pieces = []
    for h in range(acc.shape[1] // LANES):
        x = acc[:, h * LANES:(h + 1) * LANES]
        sq = x * x
        s_lo = jnp.sum(jnp.where(lo, sq, 0.0), axis=-1, keepdims=True)
        s_hi = jnp.sum(jnp.where(lo, 0.0, sq), axis=-1, keepdims=True)
        ms = jnp.where(lo, s_lo, s_hi) * (2.0 / LANES)
        y = x * lax.rsqrt(ms + NORM_EPS) * gain
        pieces.append(y if scale == 1.0 else y * scale)
    outs[0][...] = jnp.concatenate(pieces, axis=1).astype(outs[0].dtype)


def _epi_plain2d(acc, extras, outs):
    outs[0][...] = acc.astype(outs[0].dtype)


def _epi_sigmoid(acc, extras, outs):
    outs[0][...] = jax.nn.sigmoid(acc).astype(outs[0].dtype)


def _epi_relu2(acc, extras, outs):
    r = jnp.maximum(acc, 0.0)
    outs[0][...] = (r * r).astype(outs[0].dtype)


def _epi_residual(acc, extras, outs):
    x_ref, gate_ref = extras
    outs[0][...] = x_ref[...] + gate_ref[...] * acc


def _lf_kernel(w_ref, h_ref, b_ref, o_ref):
    u = lax.dot_general(w_ref[...], h_ref[...], NT_DIMS, preferred_element_type=F32)
    o_ref[...] = _log_sigmoid(u + b_ref[...])


def _lf_proj(w_fa_t, h, b_fa):
    m, d_model = h.shape
    tm = _pick(m, 512)
    return pl.pallas_call(
        _lf_kernel,
        grid=(m // tm,),
        in_specs=[pl.BlockSpec((HEAD_PAD, d_model), lambda i: (0, 0)),
                  pl.BlockSpec((tm, d_model), lambda i: (i, 0)),
                  pl.BlockSpec((HEAD_PAD, 1), lambda i: (0, 0))],
        out_specs=pl.BlockSpec((HEAD_PAD, tm), lambda i: (0, i)),
        out_shape=jax.ShapeDtypeStruct((HEAD_PAD, m), F32),
        compiler_params=_cparams(1, 32),
        name="proj_logf",
    )(w_fa_t, h, b_fa)


def _ecum_kernel(x_ref, o_ref, loc_sc, tot_sc, *, n_chunks):
    x = x_ref[...].reshape(n_chunks * HEAD_PAD, LANES)
    u = _suffix_ones(LANES)
    hi, mid, lo = _split3(x)
    loc = (jnp.dot(hi, u, preferred_element_type=F32) + jnp.dot(mid, u, preferred_element_type=F32)
           + jnp.dot(lo, u, preferred_element_type=F32))
    loc_sc[...] = loc.reshape(n_chunks, HEAD_PAD, LANES)
    tot_sc[...] = jnp.sum(x, axis=-1, keepdims=True).reshape(n_chunks, HEAD_PAD, 1)

    def body(i, later):
        j = n_chunks - 1 - i
        o_ref[j] = loc_sc[j] + later
        return later + tot_sc[j]

    lax.fori_loop(0, n_chunks, body, jnp.zeros((HEAD_PAD, 1), F32))


def _ecum(x):
    nb, n_chunks = x.shape[0], x.shape[1]
    return pl.pallas_call(
        functools.partial(_ecum_kernel, n_chunks=n_chunks),
        grid=(nb,),
        in_specs=[pl.BlockSpec((None, n_chunks, HEAD_PAD, LANES), lambda b: (b, 0, 0, 0))],
        out_specs=pl.BlockSpec((None, n_chunks, HEAD_PAD, LANES), lambda b: (b, 0, 0, 0)),
        out_shape=jax.ShapeDtypeStruct(x.shape, F32),
        scratch_shapes=[pltpu.VMEM((n_chunks, HEAD_PAD, LANES), F32), pltpu.VMEM((n_chunks, HEAD_PAD, 1), F32)],
        compiler_params=_cparams(1, 32),
        name="logf_suffix_sum",
    )(x)


def _lf_gather_kernel(pt_ref, tab_ref, new_ref, o_ref, *, n_pages, n_heads):
    b = pl.program_id(0)
    sub = lax.broadcasted_iota(jnp.int32, (SUBLANES, LANES), 0)

    def body(j, carry):
        page = pt_ref[b, j]
        blk = lax.shift_right_logical(page, 3)
        row = jnp.bitwise_and(page, SUBLANES - 1)
        for h in range(n_heads):
            tile = tab_ref[h, blk]
            o_ref[j, h:h + 1, :] = jnp.sum(jnp.where(sub == row, tile, 0.0), axis=0, keepdims=True)
        o_ref[j, n_heads:HEAD_PAD, :] = jnp.zeros((HEAD_PAD - n_heads, LANES), F32)
        return carry

    lax.fori_loop(0, n_pages, body, 0)
    o_ref[n_pages] = new_ref[...]


def _lf_gather(page_table, table, lf_new):
    nb, n_pages = page_table.shape
    n_heads = table.shape[0]
    grid_spec = pltpu.PrefetchScalarGridSpec(
        num_scalar_prefetch=1,
        grid=(nb,),
        in_specs=[pl.BlockSpec(table.shape, lambda b, pt: (0, 0, 0, 0)),
                  pl.BlockSpec((None, HEAD_PAD, LANES), lambda b, pt: (b, 0, 0))],
        out_specs=pl.BlockSpec((None, n_pages + 1, HEAD_PAD, LANES), lambda b, pt: (b, 0, 0, 0)))
    return pl.pallas_call(
        functools.partial(_lf_gather_kernel, n_pages=n_pages, n_heads=n_heads),
        grid_spec=grid_spec,
        out_shape=jax.ShapeDtypeStruct((nb, n_pages + 1, HEAD_PAD, LANES), F32),
        compiler_params=_cparams(1, 40),
        name="logf_page_gather",
    )(page_table, table, lf_new)


def _softmax_step(s, v, carry):
    m, l, acc = carry
    m_new = jnp.maximum(m, jnp.max(s, axis=-1, keepdims=True))
    alpha = jnp.exp(m - m_new)
    p = jnp.exp(s - m_new)
    l = alpha * l + jnp.sum(p, axis=-1, keepdims=True)
    acc = alpha * acc + jnp.dot(p.astype(BF16), v, preferred_element_type=F32)
    return m_new, l, acc


def _causal_blocks(qi, tq, tk):
    return (qi * tq) // tk, -(-((qi + 1) * tq) // tk)


def _fox_prompt_kernel(q_ref, k_ref, v_ref, e_ref, o_ref, *, seq, tq, tk):
    for qi in range(seq // tq):
        q = q_ref[pl.ds(qi * tq, tq), :]

        def step(ki, carry, masked, q=q, qi=qi):
            k0 = pl.multiple_of(ki * tk, tk)
            k = k_ref[pl.ds(k0, tk), :].astype(BF16)
            v = v_ref[pl.ds(k0, tk), :].astype(BF16)
            s = lax.dot_general(q, k, NT_DIMS, preferred_element_type=F32) + e_ref[:, pl.ds(k0, tk)]
            if masked:
                row = lax.broadcasted_iota(jnp.int32, (tq, tk), 0) + qi * tq
                col = lax.broadcasted_iota(jnp.int32, (tq, tk), 1) + k0
                s = jnp.where(col <= row, s, NEG_INF)
            return _softmax_step(s, v, carry)

        n_full, n_end = _causal_blocks(qi, tq, tk)
        carry = (jnp.full((tq, 1), NEG_INF, F32), jnp.zeros((tq, 1), F32), jnp.zeros((tq, LANES), F32))
        for kd in range(n_full, n_end):
            carry = step(kd, carry, True)
        carry = lax.fori_loop(0, n_full, functools.partial(step, masked=False), carry)
        _, l, acc = carry
        o_ref[pl.ds(qi * tq, tq), :] = (acc / l).astype(o_ref.dtype)


def _fox_prompt(q, k, v, e_rows, tq, tk):
    nb, nh, seq, _ = q.shape
    hm = lambda b, h: (b, h, 0, 0)
    return pl.pallas_call(
        functools.partial(_fox_prompt_kernel, seq=seq, tq=tq, tk=tk),
        grid=(nb, nh),
        in_specs=[pl.BlockSpec((None, None, seq, LANES), hm),
                  pl.BlockSpec((None, None, seq, LANES), hm),
                  pl.BlockSpec((None, None, seq, LANES), hm),
                  pl.BlockSpec((None, None, 1, seq), hm)],
        out_specs=pl.BlockSpec((seq, LANES), lambda b, h: (b, h)),
        out_shape=jax.ShapeDtypeStruct((nb * seq, nh * LANES), BF16),
        compiler_params=_cparams(2, 32),
        name="fox_prompt_attn",
    )(q, k, v, e_rows)


def _diff_lambda(lq1, lk1, lq2, lk2, lam_init):
    return (jnp.exp(jnp.sum(lq1[...] * lk1[...], axis=-1, keepdims=True))
            - jnp.exp(jnp.sum(lq2[...] * lk2[...], axis=-1, keepdims=True)) + lam_init)


def _diff_finish(acc0, l0, acc1, l1, lam, gain, lam_init):
    o = acc0 / l0 - lam * (acc1 / l1)
    o = o * lax.rsqrt(jnp.mean(o * o, axis=-1, keepdims=True) + NORM_EPS) * gain
    return o * (1.0 - lam_init)


def _diff_prompt_kernel(q_ref, k_ref, v_ref, lq1, lk1, lq2, lk2, gs_ref, o_ref, *, seq, tq, tk, lam_init, n_heads):
    h = pl.program_id(1)
    slope = jnp.exp2(-(8.0 / n_heads) * (h + 1).astype(F32) * jnp.ones((1, tk), F32))
    lam = _diff_lambda(lq1, lk1, lq2, lk2, lam_init)
    lo = lax.broadcasted_iota(jnp.int32, (1, LANES), 1) < (LANES // 2)
    for qi in range(seq // tq):
        q = q_ref[pl.ds(qi * tq, tq), :]
        zero = jnp.zeros_like(q)
        q2 = jnp.concatenate([jnp.where(lo, q, zero), jnp.where(lo, zero, q)], axis=0)

        def step(ki, carry, masked, q2=q2, qi=qi):
            k0 = pl.multiple_of(ki * tk, tk)
            k = k_ref[pl.ds(k0, tk), :].astype(BF16)
            v = v_ref[pl.ds(k0, tk), :].astype(BF16)
            kpos = (lax.broadcasted_iota(jnp.int32, (1, tk), 1) + (k0 - qi * tq)).astype(F32)
            s = lax.dot_general(q2, k, NT_DIMS, preferred_element_type=F32) + slope * kpos
            if masked:
                row = lax.broadcasted_iota(jnp.int32, (2 * tq, tk), 0)
                row = jnp.where(row >= tq, row - tq, row) + qi * tq
                col = lax.broadcasted_iota(jnp.int32, (2 * tq, tk), 1) + k0
                s = jnp.where(col <= row, s, NEG_INF)
            return _softmax_step(s, v, carry)

        n_full, n_end = _causal_blocks(qi, tq, tk)
        carry = (jnp.full((2 * tq, 1), NEG_INF, F32), jnp.zeros((2 * tq, 1), F32), jnp.zeros((2 * tq, LANES), F32))
        for kd in range(n_full, n_end):
            carry = step(kd, carry, True)
        carry = lax.fori_loop(0, n_full, functools.partial(step, masked=False), carry)
        _, l, acc = carry
        out = _diff_finish(acc[:tq], l[:tq], acc[tq:], l[tq:], lam, gs_ref[...], lam_init)
        o_ref[pl.ds(qi * tq, tq), :] = out.astype(o_ref.dtype)


def _diff_prompt(q, k, v, lams, g_sub, nb, seq, tq, tk, lam_init):
    nh = q.shape[1] // LANES
    blk = pl.BlockSpec((seq, LANES), lambda b, h: (b, h))
    vec64 = pl.BlockSpec((1, LANES // 2), lambda b, h: (0, 0))
    return pl.pallas_call(
        functools.partial(_diff_prompt_kernel, seq=seq, tq=tq, tk=tk, lam_init=lam_init, n_heads=nh),
        grid=(nb, nh),
        in_specs=[blk, blk, blk, vec64, vec64, vec64, vec64, pl.BlockSpec((1, LANES), lambda b, h: (0, 0))],
        out_specs=blk,
        out_shape=jax.ShapeDtypeStruct(q.shape, BF16),
        compiler_params=_cparams(2, 32),
        name="diff_prompt_attn",
    )(q, k, v, *lams, g_sub)


def _sb_block(z, v, later, acc, u, valid):
    l1mb = -jnp.maximum(z, 0.0) - jnp.log(1.0 + jnp.exp(-jnp.abs(z)))
    if valid is not None:
        l1mb = jnp.where(valid, l1mb, 0.0)
    after = _suffix_sum(l1mb, u) + later
    a = jnp.exp(l1mb + z + after)
    if valid is not None:
        a = jnp.where(valid, a, 0.0)
    acc = acc + jnp.dot(a.astype(BF16), v, preferred_element_type=F32)
    return later + jnp.sum(l1mb, axis=-1, keepdims=True), acc


def _sb_prompt_kernel(q_ref, k_ref, v_ref, o_ref, *, seq, tq, tk):
    u = _suffix_ones(tk)
    for qi in range(seq // tq):
        q = q_ref[pl.ds(qi * tq, tq), :]

        def step(ki, carry, masked, q=q, qi=qi):
            k0 = pl.multiple_of(ki * tk, tk)
            k = k_ref[pl.ds(k0, tk), :].astype(BF16)
            v = v_ref[pl.ds(k0, tk), :].astype(BF16)
            z = lax.dot_general(q, k, NT_DIMS, preferred_element_type=F32)
            valid = None
            if masked:
                row = lax.broadcasted_iota(jnp.int32, (tq, tk), 0) + qi * tq
                col = lax.broadcasted_iota(jnp.int32, (tq, tk), 1) + k0
                valid = col < row
            return _sb_block(z, v, carry[0], carry[1], u, valid)

        n_full, n_end = _causal_blocks(qi, tq, tk)
        carry = (jnp.zeros((tq, 1), F32), jnp.zeros((tq, LANES), F32))
        for kd in reversed(range(n_full, n_end)):
            carry = step(kd, carry, True)
        carry = lax.fori_loop(0, n_full, lambda i, c: step(n_full - 1 - i, c, False), carry)
        o_ref[pl.ds(qi * tq, tq), :] = carry[1].astype(o_ref.dtype)


def _sb_prompt(q, k, v, tq, tk):
    nb, nh, seq, _ = q.shape
    hm = lambda b, h: (b, h, 0, 0)
    return pl.pallas_call(
        functools.partial(_sb_prompt_kernel, seq=seq, tq=tq, tk=tk),
        grid=(nb, nh),
        in_specs=[pl.BlockSpec((None, None, seq, LANES), hm)] * 3,
        out_specs=pl.BlockSpec((seq, LANES), lambda b, h: (b, h)),
        out_shape=jax.ShapeDtypeStruct((nb * seq, nh * LANES), BF16),
        compiler_params=_cparams(2, 32),
        name="sb_prompt_attn",
    )(q, k, v)


def _page_specs(cache_block, layer, n_per_step, page_of):
    def spec(p):
        return pl.BlockSpec((None, None) + cache_block,
                            lambda b, j, pt: (layer, pt[b, page_of(j, p)]) + (0,) * len(cache_block))
    return [spec(p) for p in range(n_per_step)]


def _new_token_mask(rows):
    t = jnp.bitwise_and(lax.broadcasted_iota(jnp.int32, (rows, LANES), 0), SUBLANES - 1)
    return lax.broadcasted_iota(jnp.int32, (rows, LANES), 1), t


def _fox_dec_kernel(pt_ref, q_ref, e_ref, en_ref, kn_ref, vn_ref, *rest, n_slots, n_heads, n_steps):
    k_refs, v_refs = rest[:n_slots], rest[n_slots:2 * n_slots]
    o_ref, m_sc, l_sc, acc_sc = rest[2 * n_slots:]
    j = pl.program_id(1)
    rows = n_heads * SUBLANES

    @pl.when(j == 0)
    def _():
        m_sc[...] = jnp.full((rows, 1), NEG_INF, F32)
        l_sc[...] = jnp.zeros((rows, 1), F32)
        acc_sc[...] = jnp.zeros((rows, LANES), F32)

    def scores(k_of, e_of):
        return jnp.concatenate(
            [lax.dot_general(q_ref[h], k_of(h).astype(BF16), NT_DIMS, preferred_element_type=F32) + e_of(h)
             for h in range(n_heads)], axis=0)

    def update(s, v_of):
        m_prev = m_sc[...]
        m_new = jnp.maximum(m_prev, jnp.max(s, axis=-1, keepdims=True))
        alpha = jnp.exp(m_prev - m_new)
        p = jnp.exp(s - m_new)
        l_sc[...] = alpha * l_sc[...] + jnp.sum(p, axis=-1, keepdims=True)
        pb = p.astype(BF16)
        pv = []
        for h in range(n_heads):
            tot = None
            for pg in range(s.shape[1] // LANES):
                part = jnp.dot(pb[h * SUBLANES:(h + 1) * SUBLANES, pg * LANES:(pg + 1) * LANES],
                               v_of(pg, h).astype(BF16), preferred_element_type=F32)
                tot = part if tot is None else tot + part
            pv.append(tot)
        acc_sc[...] = alpha * acc_sc[...] + jnp.concatenate(pv, axis=0)
        m_sc[...] = m_new

    s_past = jnp.concatenate(
        [scores(lambda h, p=p: k_refs[p][h], lambda h, p=p: e_ref[p, h:h + 1, :]) for p in range(n_slots)], axis=1)
    update(s_past, lambda pg, h: v_refs[pg][h])

    @pl.when(j == n_steps - 1)
    def _():
        col, t = _new_token_mask(rows)
        s_new = scores(lambda h: kn_ref[h], lambda h: en_ref[h:h + 1, :])
        update(jnp.where(col <= t, s_new, NEG_INF), lambda pg, h: vn_ref[h])
        o_ref[...] = (acc_sc[...] / l_sc[...]).reshape(n_heads, SUBLANES, LANES).astype(o_ref.dtype)


def _fox_decode(page_table, q, e_all, k_new, v_new, cache_k, cache_v, layer, n_slots):
    nb, n_pages = page_table.shape
    n_heads = q.shape[1]
    n_steps = n_pages // n_slots
    page_blk = (n_heads, LANES, LANES)
    page_of = lambda j, p: j * n_slots + p
    per_b = lambda b, j, pt: (b, 0, 0, 0)
    grid_spec = pltpu.PrefetchScalarGridSpec(
        num_scalar_prefetch=1,
        grid=(nb, n_steps),
        in_specs=[pl.BlockSpec((None, n_heads, SUBLANES, LANES), per_b),
                  pl.BlockSpec((None, n_slots, HEAD_PAD, LANES), lambda b, j, pt: (b, j, 0, 0)),
                  pl.BlockSpec((None, None, HEAD_PAD, LANES), lambda b, j, pt: (b, n_pages, 0, 0)),
                  pl.BlockSpec((None,) + page_blk, per_b),
                  pl.BlockSpec((None,) + page_blk, per_b)]
                 + _page_specs(page_blk, layer, n_slots, page_of) + _page_specs(page_blk, layer, n_slots, page_of),
        out_specs=pl.BlockSpec((None, n_heads, SUBLANES, LANES), per_b),
        scratch_shapes=[pltpu.VMEM((n_heads * SUBLANES, 1), F32), pltpu.VMEM((n_heads * SUBLANES, 1), F32),
                        pltpu.VMEM((n_heads * SUBLANES, LANES), F32)])
    return pl.pallas_call(
        functools.partial(_fox_dec_kernel, n_slots=n_slots, n_heads=n_heads, n_steps=n_steps),
        grid_spec=grid_spec,
        out_shape=jax.ShapeDtypeStruct((nb, n_heads, SUBLANES, LANES), BF16),
        compiler_params=_cparams(2, 48),
        name="fox_decode_attn",
    )(page_table, q, e_all, e_all, k_new, v_new, *([cache_k] * n_slots), *([cache_v] * n_slots))


def _diff_dec_kernel(pt_ref, q_ref, kn_ref, vn_ref, lq1, lk1, lq2, lk2, gs_ref, *rest,
                     n_slots, n_heads, n_steps, past, lam_init):
    k_refs, v_refs = rest[:n_slots], rest[n_slots:2 * n_slots]
    o_ref, m_sc, l_sc, acc_sc = rest[2 * n_slots:]
    j = pl.program_id(1)
    hr = 2 * SUBLANES
    rows = n_heads * hr
    lo = lax.broadcasted_iota(jnp.int32, (1, LANES), 1) < (LANES // 2)

    @pl.when(j == 0)
    def _():
        m_sc[...] = jnp.full((rows, 1), NEG_INF, F32)
        l_sc[...] = jnp.zeros((rows, 1), F32)
        acc_sc[...] = jnp.zeros((rows, LANES), F32)

    col, t = _new_token_mask(hr)

    def scores(kt_of, first_pos):
        dist = ((past + t) - (first_pos + col)).astype(F32)
        out = []
        for h in range(n_heads):
            q = q_ref[h]
            zero = jnp.zeros_like(q)
            q2 = jnp.concatenate([jnp.where(lo, q, zero), jnp.where(lo, zero, q)], axis=0)
            s = jnp.dot(q2, kt_of(h).astype(BF16), preferred_element_type=F32)
            out.append(s - (2.0 ** (-(8.0 / n_heads) * (h + 1))) * dist)
        return jnp.concatenate(out, axis=0)

    ncol = LANES * n_heads
    slot_of_col, _ = _div_mod(lax.broadcasted_iota(jnp.int32, (LANES, ncol), 1), n_heads)
    spread = jnp.where(slot_of_col == lax.broadcasted_iota(jnp.int32, (LANES, ncol), 0), 1.0, 0.0).astype(BF16)
    head_of_row, _ = _div_mod(lax.broadcasted_iota(jnp.int32, (rows, ncol), 0), hr)
    own_head = head_of_row == _div_mod(lax.broadcasted_iota(jnp.int32, (rows, ncol), 1), n_heads)[1]

    def update(s, v_of):
        m_prev = m_sc[...]
        m_new = jnp.maximum(m_prev, jnp.max(s, axis=-1, keepdims=True))
        alpha = jnp.exp(m_prev - m_new)
        p = jnp.exp(s - m_new)
        l_sc[...] = alpha * l_sc[...] + jnp.sum(p, axis=-1, keepdims=True)
        pb = p.astype(BF16)
        tot = None
        for pg in range(s.shape[1] // LANES):
            wide = jnp.dot(pb[:, pg * LANES:(pg + 1) * LANES], spread, preferred_element_type=F32)
            wide = jnp.where(own_head, wide, 0.0).astype(BF16)
            part = jnp.dot(wide, v_of(pg).astype(BF16), preferred_element_type=F32)
            tot = part if tot is None else tot + part
        acc_sc[...] = alpha * acc_sc[...] + tot
        m_sc[...] = m_new

    s_past = jnp.concatenate(
        [scores(lambda h, p=p: k_refs[p][h], (j * n_slots + p) * LANES) for p in range(n_slots)], axis=1)
    update(s_past, lambda pg: v_refs[pg][...].reshape(ncol, LANES))

    @pl.when(j == n_steps - 1)
    def _():
        s_new = scores(lambda h: kn_ref[h], past)
        valid = jnp.concatenate([col <= t] * n_heads, axis=0)
        update(jnp.where(valid, s_new, NEG_INF), lambda pg: vn_ref[...].reshape(ncol, LANES))
        lam = _diff_lambda(lq1, lk1, lq2, lk2, lam_init)
        acc, l = acc_sc[...], l_sc[...]
        for h in range(n_heads):
            r0 = h * hr
            out = _diff_finish(acc[r0:r0 + SUBLANES], l[r0:r0 + SUBLANES], acc[r0 + SUBLANES:r0 + hr],
                               l[r0 + SUBLANES:r0 + hr], lam, gs_ref[...], lam_init)
            o_ref[h] = out.astype(o_ref.dtype)


def _diff_decode(page_table, q, kt_new, v_new, lams, g_sub, cache_kt, cache_v, layer, n_slots, lam_init):
    nb, n_pages = page_table.shape
    n_heads = q.shape[1]
    n_steps = n_pages // n_slots
    kt_blk, v_blk = (n_heads, LANES, LANES), (LANES, n_heads, LANES)
    page_of = lambda j, p: j * n_slots + p
    per_b = lambda b, j, pt: (b, 0, 0, 0)
    vec64 = pl.BlockSpec((1, LANES // 2), lambda b, j, pt: (0, 0))
    rows = n_heads * 2 * SUBLANES
    grid_spec = pltpu.PrefetchScalarGridSpec(
        num_scalar_prefetch=1,
        grid=(nb, n_steps),
        in_specs=[pl.BlockSpec((None, n_heads, SUBLANES, LANES), per_b),
                  pl.BlockSpec((None,) + kt_blk, per_b),
                  pl.BlockSpec((None,) + v_blk, per_b),
                  vec64, vec64, vec64, vec64, pl.BlockSpec((1, LANES), lambda b, j, pt: (0, 0))]
                 + _page_specs(kt_blk, layer, n_slots, page_of) + _page_specs(v_blk, layer, n_slots, page_of),
        out_specs=pl.BlockSpec((None, n_heads, SUBLANES, LANES), per_b),
        scratch_shapes=[pltpu.VMEM((rows, 1), F32), pltpu.VMEM((rows, 1), F32), pltpu.VMEM((rows, LANES), F32)])
    return pl.pallas_call(
        functools.partial(_diff_dec_kernel, n_slots=n_slots, n_heads=n_heads, n_steps=n_steps,
                          past=n_pages * LANES, lam_init=lam_init),
        grid_spec=grid_spec,
        out_shape=jax.ShapeDtypeStruct((nb, n_heads, SUBLANES, LANES), BF16),
        compiler_params=_cparams(2, 48),
        name="diff_decode_attn",
    )(page_table, q, kt_new, v_new, *lams, g_sub, *([cache_kt] * n_slots), *([cache_v] * n_slots))


def _sb_dec_kernel(pt_ref, q_ref, kn_ref, vn_ref, *rest, n_slots, n_heads, n_steps):
    k_refs, v_refs = rest[:n_slots], rest[n_slots:2 * n_slots]
    o_ref, later_sc, acc_sc = rest[2 * n_slots:]
    j = pl.program_id(1)
    rows = n_heads * SUBLANES
    u = _suffix_ones(LANES)

    def pages(k_of, v_of, n, valid):
        z = jnp.concatenate(
            [lax.dot_general(q_ref[h], k_of(pg, h).astype(BF16), NT_DIMS, preferred_element_type=F32)
             for pg in range(n) for h in range(n_heads)], axis=0)
        l1mb = -jnp.maximum(z, 0.0) - jnp.log(1.0 + jnp.exp(-jnp.abs(z)))
        if valid is not None:
            l1mb = jnp.where(valid, l1mb, 0.0)
        page_sum = jnp.sum(l1mb, axis=-1, keepdims=True)
        later = later_sc[...]
        laters = []
        for pg in range(n):
            laters.append(later)
            later = later + page_sum[pg * rows:(pg + 1) * rows]
        after = _suffix_sum(l1mb, u) + jnp.concatenate(laters, axis=0)
        a = jnp.exp(l1mb + z + after)
        if valid is not None:
            a = jnp.where(valid, a, 0.0)
        ab = a.astype(BF16)
        pv = []
        for h in range(n_heads):
            tot = None
            for pg in range(n):
                r0 = pg * rows + h * SUBLANES
                part = jnp.dot(ab[r0:r0 + SUBLANES], v_of(pg, h).astype(BF16), preferred_element_type=F32)
                tot = part if tot is None else tot + part
            pv.append(tot)
        acc_sc[...] += jnp.concatenate(pv, axis=0)
        later_sc[...] = later

    @pl.when(j == 0)
    def _():
        later_sc[...] = jnp.zeros((rows, 1), F32)
        acc_sc[...] = jnp.zeros((rows, LANES), F32)
        col, t = _new_token_mask(rows)
        pages(lambda pg, h: kn_ref[h], lambda pg, h: vn_ref[h], 1, col < t)

    pages(lambda pg, h: k_refs[pg][h], lambda pg, h: v_refs[pg][h], n_slots, None)

    @pl.when(j == n_steps - 1)
    def _():
        o_ref[...] = acc_sc[...].reshape(n_heads, SUBLANES, LANES).astype(o_ref.dtype)


def _sb_decode(page_table, q, k_new, v_new, cache_k, cache_v, layer, n_slots):
    nb, n_pages = page_table.shape
    n_heads = q.shape[1]
    n_steps = n_pages // n_slots
    page_blk = (n_heads, LANES, LANES)
    page_of = lambda j, p: n_pages - 1 - (j * n_slots + p)
    per_b = lambda b, j, pt: (b, 0, 0, 0)
    grid_spec = pltpu.PrefetchScalarGridSpec(
        num_scalar_prefetch=1,
        grid=(nb, n_steps),
        in_specs=[pl.BlockSpec((None, n_heads, SUBLANES, LANES), per_b),
                  pl.BlockSpec((None,) + page_blk, per_b),
                  pl.BlockSpec((None,) + page_blk, per_b)]
                 + _page_specs(page_blk, layer, n_slots, page_of) + _page_specs(page_blk, layer, n_slots, page_of),
        out_specs=pl.BlockSpec((None, n_heads, SUBLANES, LANES), per_b),
        scratch_shapes=[pltpu.VMEM((n_heads * SUBLANES, 1), F32), pltpu.VMEM((n_heads * SUBLANES, LANES), F32)])
    return pl.pallas_call(
        functools.partial(_sb_dec_kernel, n_slots=n_slots, n_heads=n_heads, n_steps=n_steps),
        grid_spec=grid_spec,
        out_shape=jax.ShapeDtypeStruct((nb, n_heads, SUBLANES, LANES), BF16),
        compiler_params=_cparams(2, 48),
        name="sb_decode_attn",
    )(page_table, q, k_new, v_new, *([cache_k] * n_slots), *([cache_v] * n_slots))


def _merge_kernel(oa, ob, oc, wa, wb, wc, g0, g1, g2, o_ref):
    pa = jnp.dot(oa[...], wa[...].astype(BF16), preferred_element_type=F32)
    pb = jnp.dot(ob[...], wb[...].astype(BF16), preferred_element_type=F32)
    pc = jnp.dot(oc[...], wc[...].astype(BF16), preferred_element_type=F32)
    o_ref[...] = (g0[...] * pa + g1[...] * pb + g2[...] * pc).astype(o_ref.dtype)


def _merge(oa, ob, oc, wa, wb, wc, layer, gates, tm, tn):
    m = oa.shape[0]
    d_model = wa.shape[2]
    nj = d_model // tn
    a_spec = lambda a: pl.BlockSpec((tm, a.shape[1]), lambda i, j: (i, 0))
    w_spec = lambda w: pl.BlockSpec((None, w.shape[1], tn), lambda i, j: (layer, 0, j))
    g_spec = lambda c: pl.BlockSpec((tm, tn), lambda i, j: (i, c * nj + j))
    return pl.pallas_call(
        _merge_kernel,
        grid=(m // tm, nj),
        in_specs=[a_spec(oa), a_spec(ob), a_spec(oc), w_spec(wa), w_spec(wb), w_spec(wc),
                  g_spec(0), g_spec(1), g_spec(2)],
        out_specs=pl.BlockSpec((tm, tn), lambda i, j: (i, j)),
        out_shape=jax.ShapeDtypeStruct((m, d_model), BF16),
        compiler_params=_cparams(2, 48),
        name="branch_merge",
    )(oa, ob, oc, wa, wb, wc, gates, gates, gates)


def _project(h, g, name, w_t, col0, ncols, tn, epilogue, extras=(), extra_specs=(), heads=None, dtype=F32):
    m, d_model = h.shape
    if heads is not None:
        shape, spec = _heads_out(g, heads, tn // LANES, dtype)
    else:
        shape = jax.ShapeDtypeStruct((m, ncols), dtype)
        spec = pl.BlockSpec((g.tm, tn), lambda i, j, k: (i, j))
    return _mm(name, h, w_t, None, col0, ncols, g.tm, tn, d_model, epilogue, list(extras), list(extra_specs),
               shape, spec)


def _vec_spec(n):
    return pl.BlockSpec((1, n), lambda i, j, k: (0, 0))


def _layer_front(x, mod, g, p, dims):
    fox_w, dqk_w, dv_w, sb_w, d_model = dims
    fh, sh = fox_w // LANES, sb_w // LANES
    h = _prenorm(x, p["g_norm1"], mod, g, 1, 0)
    w_fox, w_rest = p["w_in_fox_t"], p["w_in_rest_t"]
    tn = 512
    sc128 = 1.0 / math.sqrt(LANES)
    rms = lambda scale: functools.partial(_epi_heads_rms, g=g, scale=scale)
    plain = lambda scale: functools.partial(_epi_heads_plain, g=g, scale=scale)
    half = lambda scale: functools.partial(_epi_rms_half, scale=scale)
    out = {}
    proj = functools.partial(_project, h, g)
    out["qa"] = proj("proj_fox_q", w_fox, 0, fox_w, tn, rms(sc128), [p["g_q_fox"]], [_vec_spec(LANES)], heads=fh, dtype=BF16)
    out["ka"] = proj("proj_fox_k", w_fox, fox_w, fox_w, tn, rms(1.0), [p["g_k_fox"]], [_vec_spec(LANES)], heads=fh)
    out["va"] = proj("proj_fox_v", w_fox, 2 * fox_w, fox_w, tn, plain(1.0), heads=fh)
    c = 0
    out["qb"] = proj("proj_diff_q", w_rest, c, dqk_w, tn, half(0.125), [p["g_q_diff"]], [_vec_spec(LANES)], dtype=BF16); c += dqk_w
    out["kb"] = proj("proj_diff_k", w_rest, c, dqk_w, tn, half(1.0), [p["g_k_diff"]], [_vec_spec(LANES)]); c += dqk_w
    out["vb"] = proj("proj_diff_v", w_rest, c, dv_w, tn, _epi_plain2d); c += dv_w
    out["qc"] = proj("proj_sb_q", w_rest, c, sb_w, tn, plain(sc128), heads=sh, dtype=BF16); c += sb_w
    out["kc"] = proj("proj_sb_k", w_rest, c, sb_w, tn, plain(1.0), heads=sh); c += sb_w
    out["vc"] = proj("proj_sb_v", w_rest, c, sb_w, tn, plain(1.0), heads=sh); c += sb_w
    out["gates"] = proj("proj_gates", w_rest, c, 3 * d_model, tn, _epi_sigmoid, dtype=BF16)
    out["lf_t"] = _lf_proj(p["w_fa_t"], h, p["b_fa"])
    return out


def _layer_back(x, mod, g, p, oa, ob, oc, gates):
    m, d_model = x.shape
    layer = p["layer"]
    merged = _merge(oa, ob, oc, p["w_br_fox"], p["w_br_diff"], p["w_br_sb"], layer, gates, g.tm, 256)
    tn = 512
    xspec = pl.BlockSpec((g.tm, tn), lambda i, j, k: (i, j))
    oshape = jax.ShapeDtypeStruct((m, d_model), F32)
    x1 = _mm("out_proj", merged, p["w_out"], layer, 0, d_model, g.tm, tn, d_model, _epi_residual,
             [x, mod], [xspec, _mod_spec(g, 2, d_model, tn)], oshape, xspec)
    h2 = _prenorm(x1, p["g_norm2"], mod, g, 4, 3)
    d_ff = p["w_up"].shape[2]
    f = _mm("mlp_up", h2, p["w_up"], layer, 0, d_ff, g.tm, tn, d_model, _epi_relu2, [], [],
            jax.ShapeDtypeStruct((m, d_ff), BF16), xspec)
    tn2 = _pick(d_model, 1024)
    xspec2 = pl.BlockSpec((g.tm, tn2), lambda i, j, k: (i, j))
    return _mm("mlp_down", f, p["w_down"], layer, 0, d_model, g.tm, tn2, _pick(d_ff, 2048), _epi_residual,
               [x1, mod], [xspec2, _mod_spec(g, 5, d_model, tn2)], oshape, xspec2)


def kernel(x_prompt, x_sample, cache_fox_k, cache_fox_v, cache_fox_logf, cache_diff_k, cache_diff_v, cache_sb_k, cache_sb_v, page_table, c_prompt, c_sample, w_ada, b_ada, g_norm1, w_in, b_fgate, g_q_fox, g_k_fox, g_q_diff, g_k_diff, lambda_q1, lambda_k1, lambda_q2, lambda_k2, g_sub_diff, w_br_fox, w_br_diff, w_br_sb, w_out, g_norm2, w_up, w_down):
    n_layers = w_ada.shape[0]
    nbp, seq, d_model = x_prompt.shape
    nbs, dseq, _ = x_sample.shape
    assert dseq == SUBLANES, "decode kernels hold the new tokens of a sequence in one sublane group"
    n_pages = page_table.shape[1]
    n_pool, page_size, fh, fd = cache_fox_k.shape[1:]
    dh, dqk = cache_diff_k.shape[3], cache_diff_k.shape[5]
    sh = cache_sb_k.shape[3]
    assert page_size == LANES and fd == LANES and 2 * dqk == LANES and cache_sb_k.shape[4] == LANES
    fox_w, dqk_w, dv_w, sb_w = fh * LANES, dh * LANES, dh * LANES, sh * LANES
    dims = (fox_w, dqk_w, dv_w, sb_w, d_model)
    past = n_pages * page_size
    mp, ms = nbp * seq, nbs * dseq

    gp = Group(nbp, seq, _pick(seq, 1024), False)
    gs = Group(nbs, dseq, ms, True)
    tq_fox, tq_diff, tq_sb = _pick(seq, 1024), _pick(seq, 512), _pick(seq, 512)
    tk_fox, tk_diff, tk_sb = _pick(seq, 512), _pick(seq, 512), _pick(seq, 256)
    n_slots = max(s for s in (8, 4, 2, 1) if n_pages % s == 0)

    n_c = nbp + nbs
    c_rows = -(-n_c // SUBLANES) * SUBLANES
    c_all = jnp.concatenate([c_prompt, c_sample, jnp.zeros((c_rows - n_c, d_model), F32)], axis=0)
    mod_all = _ada(c_all, w_ada, b_ada)

    fox_k_hm = jnp.transpose(cache_fox_k, (0, 1, 3, 2, 4))
    fox_v_hm = jnp.transpose(cache_fox_v, (0, 1, 3, 2, 4))
    sb_k_hm = jnp.transpose(cache_sb_k, (0, 1, 3, 2, 4))
    sb_v_hm = jnp.transpose(cache_sb_v, (0, 1, 3, 2, 4))
    diff_kt = jnp.transpose(cache_diff_k, (0, 1, 3, 4, 5, 2)).reshape(n_layers, n_pool, dh, LANES, page_size)
    logf_t = jnp.transpose(cache_fox_logf, (0, 3, 1, 2))
    pool_pad = -(-n_pool // SUBLANES) * SUBLANES

    c_fa = 3 * fox_w
    n_fa = fh
    yp, ys = x_prompt.reshape(mp, d_model), x_sample.reshape(ms, d_model)
    outs_p, outs_s = [], []
    w_in_t = jnp.transpose(w_in, (2, 0, 1))
    w_down_bf = w_down.astype(BF16)
    for l in range(n_layers):
        lam_init = 0.8 - 0.6 * math.exp(-0.3 * l)
        pick_l = (jnp.arange(n_layers) == l)[None, :, None]
        layer_of = lambda rows: jnp.sum(jnp.where(pick_l, rows, 0.0), axis=1).astype(BF16)
        p = {
            "layer": l,
            "w_in_fox_t": layer_of(w_in_t[:c_fa]),
            "w_in_rest_t": layer_of(w_in_t[c_fa + n_fa:]),
            "w_fa_t": jnp.pad(layer_of(w_in_t[c_fa:c_fa + n_fa]), ((0, HEAD_PAD - n_fa), (0, 0))),
            "b_fa": jnp.pad(b_fgate[l], (0, HEAD_PAD - n_fa)).reshape(HEAD_PAD, 1),
            "g_norm1": g_norm1[l], "g_norm2": g_norm2[l],
            "g_q_fox": g_q_fox[l].reshape(1, LANES), "g_k_fox": g_k_fox[l].reshape(1, LANES),
            "g_q_diff": jnp.tile(g_q_diff[l], 2).reshape(1, LANES), "g_k_diff": jnp.tile(g_k_diff[l], 2).reshape(1, LANES),
            "w_br_fox": w_br_fox, "w_br_diff": w_br_diff, "w_br_sb": w_br_sb,
            "w_out": w_out, "w_up": w_up, "w_down": w_down_bf,
        }
        lams = [v[l].reshape(1, dqk) for v in (lambda_q1, lambda_k1, lambda_q2, lambda_k2)]
        g_sub = g_sub_diff[l].reshape(1, LANES)

        mod_p = mod_all[l, :nbp].reshape(nbp, 1, -1)
        fr = _layer_front(yp, mod_p, gp, p, dims)
        lf_p = fr["lf_t"]
        x_e = lf_p.reshape(HEAD_PAD, nbp, seq // LANES, LANES).transpose(1, 2, 0, 3)
        e_p = _ecum(x_e).transpose(0, 2, 1, 3).reshape(nbp, HEAD_PAD, 1, seq)
        oa = _fox_prompt(fr["qa"], fr["ka"], fr["va"], e_p, tq_fox, tk_fox)
        ob = _diff_prompt(fr["qb"], fr["kb"], fr["vb"], lams, g_sub, nbp, seq, tq_diff, tk_diff, lam_init)
        oc = _sb_prompt(fr["qc"], fr["kc"], fr["vc"], tq_sb, tk_sb)
        yp = _layer_back(yp, mod_p, gp, p, oa, ob, oc, fr["gates"])
        hm_out = lambda a: jnp.transpose(a, (0, 2, 1, 3))
        outs_p.append((
            hm_out(fr["ka"]), hm_out(fr["va"]), lf_p[:fh].reshape(fh, nbp, seq).transpose(1, 2, 0),
            fr["kb"].reshape(nbp, seq, dh, 2, dqk), fr["vb"].reshape(nbp, seq, dh, LANES),
            hm_out(fr["kc"]), hm_out(fr["vc"])))

        mod_s = jnp.repeat(mod_all[l, nbp:nbp + nbs], dseq, axis=0)
        fs = _layer_front(ys, mod_s, gs, p, dims)
        lf_s = fs["lf_t"]
        lf_new = jnp.pad(lf_s.reshape(HEAD_PAD, nbs, dseq).transpose(1, 0, 2), ((0, 0), (0, 0), (0, LANES - dseq)))
        table = jnp.pad(logf_t[l], ((0, 0), (0, pool_pad - n_pool), (0, 0))).reshape(fh, pool_pad // SUBLANES, SUBLANES, LANES)
        e_s = _ecum(_lf_gather(page_table, table, lf_new))
        pad_keys = lambda a: jnp.pad(a, ((0, 0), (0, 0), (0, LANES - dseq), (0, 0)))
        oa = _fox_decode(page_table, fs["qa"], e_s, pad_keys(fs["ka"]), pad_keys(fs["va"]), fox_k_hm, fox_v_hm, l, n_slots)
        qb_hm = fs["qb"].reshape(nbs, dseq, dh, LANES).transpose(0, 2, 1, 3)
        kb_t = fs["kb"].reshape(nbs, dseq, dh, LANES).transpose(0, 2, 3, 1)
        kt_new = jnp.pad(kb_t, ((0, 0), (0, 0), (0, 0), (0, LANES - dseq)))
        vb_new = jnp.pad(fs["vb"].reshape(nbs, dseq, dh, LANES), ((0, 0), (0, LANES - dseq), (0, 0), (0, 0)))
        ob = _diff_decode(page_table, qb_hm, kt_new, vb_new, lams, g_sub, diff_kt, cache_diff_v, l, n_slots, lam_init)
        oc = _sb_decode(page_table, fs["qc"], pad_keys(fs["kc"]), pad_keys(fs["vc"]), sb_k_hm, sb_v_hm, l, n_slots)
        rows2d = lambda a: jnp.transpose(a, (0, 2, 1, 3)).reshape(ms, -1)
        ys = _layer_back(ys, mod_s, gs, p, rows2d(oa), rows2d(ob), rows2d(oc), fs["gates"])
        outs_s.append((
            hm_out(fs["ka"]), hm_out(fs["va"]), lf_s[:fh].reshape(fh, nbs, dseq).transpose(1, 2, 0),
            fs["kb"].reshape(nbs, dseq, dh, 2, dqk), fs["vb"].reshape(nbs, dseq, dh, LANES),
            hm_out(fs["kc"]), hm_out(fs["vc"])))

    stack = lambda outs: tuple(jnp.stack([o[i] for o in outs]) for i in range(7))
    return (yp.reshape(nbp, seq, d_model), ys.reshape(nbs, dseq, d_model)) + stack(outs_p) + stack(outs_s)
```

```python
import functools
import math
from typing import NamedTuple

import jax
import jax.numpy as jnp
from jax import lax
from jax.experimental import pallas as pl
from jax.experimental.pallas import tpu as pltpu

F32 = jnp.float32
BF16 = jnp.bfloat16
NORM_EPS = 1e-6
LANES = 128
SUBLANES = 8
HEAD_PAD = 16
NT_DIMS = (((1,), (1,)), ((), ()))
MIB = 1 << 20
NEG_INF = float("-inf")


class Group(NamedTuple):
    B: int
    T: int
    tm: int
    per_row_mod: bool


def _pick(dim, pref):
    t = min(pref, dim)
    while t >= LANES:
        if dim % t == 0 and t % LANES == 0:
            return t
        t -= LANES
    return dim


def _cparams(n_axes, vmem_mib=None):
    return pltpu.CompilerParams(
        dimension_semantics=("arbitrary",) * n_axes,
        vmem_limit_bytes=None if vmem_mib is None else vmem_mib * MIB)


def _log_sigmoid(x):
    return jnp.minimum(x, 0.0) - jnp.log(1.0 + jnp.exp(-jnp.abs(x)))


def _split3(x):
    hi = x.astype(BF16)
    r1 = x - hi.astype(F32)
    mid = r1.astype(BF16)
    lo = (r1 - mid.astype(F32)).astype(BF16)
    return hi, mid, lo


def _suffix_sum(x, u):
    hi = x.astype(BF16)
    lo = (x - hi.astype(F32)).astype(BF16)
    return jnp.dot(hi, u, preferred_element_type=F32) + jnp.dot(lo, u, preferred_element_type=F32)


def _div_mod(x, n):
    if n & (n - 1) == 0:
        return lax.shift_right_logical(x, n.bit_length() - 1), jnp.bitwise_and(x, n - 1)
    return lax.div(x, n), lax.rem(x, n)


def _suffix_ones(n):
    r = lax.broadcasted_iota(jnp.int32, (n, n), 0)
    c = lax.broadcasted_iota(jnp.int32, (n, n), 1)
    return (r > c).astype(BF16)


def _mod_spec(g, chunk, d_model, tn):
    nj = d_model // tn
    if g.per_row_mod:
        return pl.BlockSpec((g.tm, tn), lambda i, j, *_: (i, chunk * nj + j))
    nt = g.T // g.tm
    return pl.BlockSpec((None, 1, tn), lambda i, j, *_: (i // nt, 0, chunk * nj + j))


def _ada_kernel(c_ref, w_ref, b_ref, o_ref):
    c = c_ref[...]
    a = (c * jax.nn.sigmoid(c)).astype(BF16)
    o_ref[...] = jnp.dot(a, w_ref[...].astype(BF16), preferred_element_type=F32) + b_ref[...]


def _ada(c_all, w_ada, b_ada):
    n_layers, d_model, n = w_ada.shape
    rows = c_all.shape[0]
    tn = _pick(n, 512)
    return pl.pallas_call(
        _ada_kernel,
        grid=(n_layers, n // tn),
        in_specs=[pl.BlockSpec((rows, d_model), lambda l, j: (0, 0)),
                  pl.BlockSpec((None, d_model, tn), lambda l, j: (l, 0, j)),
                  pl.BlockSpec((None, 1, tn), lambda l, j: (l, 0, j))],
        out_specs=pl.BlockSpec((None, rows, tn), lambda l, j: (l, 0, j)),
        out_shape=jax.ShapeDtypeStruct((n_layers, rows, n), F32),
        compiler_params=_cparams(2, 40),
        name="ada_modulation",
    )(c_all, w_ada, b_ada.reshape(n_layers, 1, n))


def _w_in_split_kernel(w_ref, o_ref, *, tn, n_layers, n_chunks):
    per_col = n_chunks * n_layers
    for l in range(n_layers):
        for kc in range(n_chunks):
            rows = w_ref[pl.ds(kc * n_layers + l, tn, stride=per_col), :]
            o_ref[l, :, kc * LANES:(kc + 1) * LANES] = rows.astype(o_ref.dtype)


def _w_in_by_layer(w_in, skip0, n_skip, tn):
    n_layers, kdim, n = w_in.shape
    n_chunks = kdim // LANES
    per_col = n_chunks * n_layers
    rows = w_in.reshape(n_layers, n_chunks, LANES, n).transpose(3, 1, 0, 2).reshape(n * per_col, LANES)
    n_out = n - n_skip
    assert skip0 % tn == 0 and n_out % tn == 0

    def src(i):
        col = i * tn
        return (jnp.where(col >= skip0, col + n_skip, col) * per_col, 0)

    return pl.pallas_call(
        functools.partial(_w_in_split_kernel, tn=tn, n_layers=n_layers, n_chunks=n_chunks),
        grid=(n_out // tn,),
        in_specs=[pl.BlockSpec((pl.Element(tn * per_col), pl.Element(LANES)), src)],
        out_specs=pl.BlockSpec((n_layers, tn, kdim), lambda i: (0, i, 0)),
        out_shape=jax.ShapeDtypeStruct((n_layers, n_out, kdim), BF16),
        compiler_params=_cparams(1, 40),
        name="w_in_by_layer",
    )(rows)


def _prenorm_kernel(x_ref, g_ref, sc_ref, sh_ref, o_ref):
    x = x_ref[...]
    ms = jnp.mean(x * x, axis=-1, keepdims=True)
    y = x * lax.rsqrt(ms + NORM_EPS) * g_ref[...]
    o_ref[...] = (y * (1.0 + sc_ref[...]) + sh_ref[...]).astype(o_ref.dtype)


def _prenorm(x, gain, mod, g, scale_chunk, shift_chunk):
    m, d_model = x.shape
    tm = min(g.tm, 256)
    gg = g._replace(tm=tm)
    return pl.pallas_call(
        _prenorm_kernel,
        grid=(m // tm, 1),
        in_specs=[pl.BlockSpec((tm, d_model), lambda i, j: (i, 0)),
                  pl.BlockSpec((1, d_model), lambda i, j: (0, 0)),
                  _mod_spec(gg, scale_chunk, d_model, d_model),
                  _mod_spec(gg, shift_chunk, d_model, d_model)],
        out_specs=pl.BlockSpec((tm, d_model), lambda i, j: (i, 0)),
        out_shape=jax.ShapeDtypeStruct((m, d_model), BF16),
        compiler_params=_cparams(2, 40),
        name="mod_rmsnorm",
    )(x, gain.reshape(1, d_model), mod, mod)


def _mm_kernel(*refs, nk, n_extra, n_aliased, epilogue, w_rows_are_outputs):
    a_ref, w_ref = refs[0], refs[1]
    extras = refs[2:2 + n_extra]
    refs = refs[:2 + n_extra] + refs[2 + n_extra + n_aliased:]
    if w_rows_are_outputs:
        part = lax.dot_general(a_ref[...], w_ref[...].astype(BF16), NT_DIMS, preferred_element_type=F32)
    else:
        part = jnp.dot(a_ref[...], w_ref[...].astype(BF16), preferred_element_type=F32)
    if nk == 1:
        epilogue(part, extras, refs[2 + n_extra:])
        return
    outs, acc_ref = refs[2 + n_extra:-1], refs[-1]
    k = pl.program_id(2)

    @pl.when(k == 0)
    def _():
        acc_ref[...] = part

    @pl.when(k > 0)
    def _():
        acc_ref[...] += part

    @pl.when(k == nk - 1)
    def _():
        epilogue(acc_ref[...], extras, outs)


def _mm(name, a, w, layer, col0, ncols, tm, tn, tk, epilogue, extras, extra_specs, out_shape, out_specs,
        w_rows_are_outputs=False, into=None, vmem_mib=48):
    m, kdim = a.shape
    assert col0 % tn == 0 and ncols % tn == 0 and m % tm == 0 and kdim % tk == 0
    nk, c0 = kdim // tk, col0 // tn
    if w_rows_are_outputs:
        w_spec = pl.BlockSpec((None, tn, tk), lambda i, j, k: (layer, c0 + j, k))
    else:
        w_spec = pl.BlockSpec((None, tk, tn), lambda i, j, k: (layer, k, c0 + j))
    aliased = [] if into is None else [into]
    kern = functools.partial(_mm_kernel, nk=nk, n_extra=len(extras), n_aliased=len(aliased), epilogue=epilogue,
                             w_rows_are_outputs=w_rows_are_outputs)
    return pl.pallas_call(
        kern,
        grid=(m // tm, ncols // tn, nk),
        in_specs=[pl.BlockSpec((tm, tk), lambda i, j, k: (i, k)), w_spec] + list(extra_specs)
                 + [pl.BlockSpec(memory_space=pl.ANY)] * len(aliased),
        out_specs=out_specs,
        out_shape=out_shape,
        input_output_aliases={2 + len(extras): 0} if aliased else {},
        scratch_shapes=[pltpu.VMEM((tm, tn), F32)] if nk > 1 else [],
        compiler_params=_cparams(3, vmem_mib),
        name=name,
    )(a, w, *extras, *aliased)


def _store_heads(val, o_ref, g):
    for h in range(val.shape[1] // LANES):
        piece = val[:, h * LANES:(h + 1) * LANES].astype(o_ref.dtype)
        if g.per_row_mod:
            o_ref[:, h] = piece.reshape(g.B, g.T, LANES)
        else:
            o_ref[h] = piece


def _heads_out(g, n_heads, nh_tile, dtype, stack=None):
    shape = jax.ShapeDtypeStruct((g.B, n_heads, g.T, LANES), dtype)
    if g.per_row_mod:
        assert stack is None
        return shape, pl.BlockSpec((g.B, nh_tile, g.T, LANES), lambda i, j, k: (0, j, 0, 0))
    nt = g.T // g.tm
    if stack is None:
        return shape, pl.BlockSpec((None, nh_tile, g.tm, LANES), lambda i, j, k: (i // nt, j, i % nt, 0))
    n_layers, layer = stack
    shape = jax.ShapeDtypeStruct((n_layers,) + shape.shape, dtype)
    return shape, pl.BlockSpec((None, None, nh_tile, g.tm, LANES), lambda i, j, k: (layer, i // nt, j, i % nt, 0))


def _epi_heads_plain(acc, extras, outs, *, g, scale):
    _store_heads(acc if scale == 1.0 else acc * scale, outs[0], g)


def _epi_heads_rms(acc, extras, outs, *, g, scale):
    gain = extras[0][...]
    pieces = []
    for h in range(acc.shape[1] // LANES):
        x = acc[:, h * LANES:(h + 1) * LANES]
        y = x * lax.rsqrt(jnp.mean(x * x, axis=-1, keepdims=True) + NORM_EPS) * gain
        pieces.append(y if scale == 1.0 else y * scale)
    _store_heads(jnp.concatenate(pieces, axis=1), outs[0], g)


def _epi_rms_half(acc, extras, outs, *, scale):
    gain = extras[0][...]
    lo = lax.broadcasted_iota(jnp.int32, (1, LANES), 1) < (LANES // 2)
    pieces = []
    for h in range(acc.shape[1] // LANES):
        x = acc[:, h * LANES:(h + 1) * LANES]
        sq = x * x
        s_lo = jnp.sum(jnp.where(lo, sq, 0.0), axis=-1, keepdims=True)
        s_hi = jnp.sum(jnp.where(lo, 0.0, sq), axis=-1, keepdims=True)
        ms = jnp.where(lo, s_lo, s_hi) * (2.0 / LANES)
        y = x * lax.rsqrt(ms + NORM_EPS) * gain
        pieces.append(y if scale == 1.0 else y * scale)
    outs[0][...] = jnp.concatenate(pieces, axis=1).astype(outs[0].dtype)


def _epi_plain2d(acc, extras, outs):
    outs[0][...] = acc.astype(outs[0].dtype)


def _epi_sigmoid(acc, extras, outs):
    outs[0][...] = jax.nn.sigmoid(acc).astype(outs[0].dtype)


def _epi_relu2(acc, extras, outs):
    r = jnp.maximum(acc, 0.0)
    outs[0][...] = (r * r).astype(outs[0].dtype)


def _epi_residual(acc, extras, outs):
    x_ref, gate_ref = extras
    outs[0][...] = x_ref[...] + gate_ref[...] * acc


def _lf_kernel(w_ref, h_ref, b_ref, o_ref):
    u = lax.dot_general(w_ref[...], h_ref[...], NT_DIMS, preferred_element_type=F32)
    o_ref[...] = _log_sigmoid(u + b_ref[...])


def _lf_proj(w_fa_t, h, b_fa):
    m, d_model = h.shape
    tm = _pick(m, 512)
    return pl.pallas_call(
        _lf_kernel,
        grid=(m // tm,),
        in_specs=[pl.BlockSpec((HEAD_PAD, d_model), lambda i: (0, 0)),
                  pl.BlockSpec((tm, d_model), lambda i: (i, 0)),
                  pl.BlockSpec((HEAD_PAD, 1), lambda i: (0, 0))],
        out_specs=pl.BlockSpec((HEAD_PAD, tm), lambda i: (0, i)),
        out_shape=jax.ShapeDtypeStruct((HEAD_PAD, m), F32),
        compiler_params=_cparams(1, 32),
        name="proj_logf",
    )(w_fa_t, h, b_fa)


def _ecum_kernel(x_ref, o_ref, loc_sc, tot_sc, *, n_chunks):
    x = x_ref[...].reshape(n_chunks * HEAD_PAD, LANES)
    u = _suffix_ones(LANES)
    hi, mid, lo = _split3(x)
    loc = (jnp.dot(hi, u, preferred_element_type=F32) + jnp.dot(mid, u, preferred_element_type=F32)
           + jnp.dot(lo, u, preferred_element_type=F32))
    loc_sc[...] = loc.reshape(n_chunks, HEAD_PAD, LANES)
    tot_sc[...] = jnp.sum(x, axis=-1, keepdims=True).reshape(n_chunks, HEAD_PAD, 1)

    def body(i, later):
        j = n_chunks - 1 - i
        o_ref[j] = loc_sc[j] + later
        return later + tot_sc[j]

    lax.fori_loop(0, n_chunks, body, jnp.zeros((HEAD_PAD, 1), F32))


def _ecum(x):
    nb, n_chunks = x.shape[0], x.shape[1]
    return pl.pallas_call(
        functools.partial(_ecum_kernel, n_chunks=n_chunks),
        grid=(nb,),
        in_specs=[pl.BlockSpec((None, n_chunks, HEAD_PAD, LANES), lambda b: (b, 0, 0, 0))],
        out_specs=pl.BlockSpec((None, n_chunks, HEAD_PAD, LANES), lambda b: (b, 0, 0, 0)),
        out_shape=jax.ShapeDtypeStruct(x.shape, F32),
        scratch_shapes=[pltpu.VMEM((n_chunks, HEAD_PAD, LANES), F32), pltpu.VMEM((n_chunks, HEAD_PAD, 1), F32)],
        compiler_params=_cparams(1, 32),
        name="logf_suffix_sum",
    )(x)


def _lf_gather_kernel(pt_ref, tab_ref, new_ref, o_ref, *, n_pages, n_heads):
    b = pl.program_id(0)
    sub = lax.broadcasted_iota(jnp.int32, (SUBLANES, LANES), 0)

    def body(j, carry):
        page = pt_ref[b, j]
        blk = lax.shift_right_logical(page, 3)
        row = jnp.bitwise_and(page, SUBLANES - 1)
        for h in range(n_heads):
            tile = tab_ref[h, blk]
            o_ref[j, h:h + 1, :] = jnp.sum(jnp.where(sub == row, tile, 0.0), axis=0, keepdims=True)
        o_ref[j, n_heads:HEAD_PAD, :] = jnp.zeros((HEAD_PAD - n_heads, LANES), F32)
        return carry

    lax.fori_loop(0, n_pages, body, 0)
    o_ref[n_pages] = new_ref[...]


def _lf_gather(page_table, table, lf_new):
    nb, n_pages = page_table.shape
    n_heads = table.shape[0]
    grid_spec = pltpu.PrefetchScalarGridSpec(
        num_scalar_prefetch=1,
        grid=(nb,),
        in_specs=[pl.BlockSpec(table.shape, lambda b, pt: (0, 0, 0, 0)),
                  pl.BlockSpec((None, HEAD_PAD, LANES), lambda b, pt: (b, 0, 0))],
        out_specs=pl.BlockSpec((None, n_pages + 1, HEAD_PAD, LANES), lambda b, pt: (b, 0, 0, 0)))
    return pl.pallas_call(
        functools.partial(_lf_gather_kernel, n_pages=n_pages, n_heads=n_heads),
        grid_spec=grid_spec,
        out_shape=jax.ShapeDtypeStruct((nb, n_pages + 1, HEAD_PAD, LANES), F32),
        compiler_params=_cparams(1, 40),
        name="logf_page_gather",
    )(page_table, table, lf_new)


def _softmax_step(s, v, carry):
    m, l, acc = carry
    m_new = jnp.maximum(m, jnp.max(s, axis=-1, keepdims=True))
    alpha = jnp.exp(m - m_new)
    p = jnp.exp(s - m_new)
    l = alpha * l + jnp.sum(p, axis=-1, keepdims=True)
    acc = alpha * acc + jnp.dot(p.astype(BF16), v, preferred_element_type=F32)
    return m_new, l, acc


def _causal_blocks(qi, tq, tk):
    return (qi * tq) // tk, -(-((qi + 1) * tq) // tk)


def _fox_prompt_kernel(q_ref, k_ref, v_ref, e_ref, o_ref, *, seq, tq, tk):
    for qi in range(seq // tq):
        q = q_ref[pl.ds(qi * tq, tq), :]

        def step(ki, carry, masked, q=q, qi=qi):
            k0 = ki * tk
            k = k_ref[pl.ds(k0, tk), :].astype(BF16)
            v = v_ref[pl.ds(k0, tk), :].astype(BF16)
            s = lax.dot_general(q, k, NT_DIMS, preferred_element_type=F32) + e_ref[:, pl.ds(k0, tk)]
            if masked:
                row = lax.broadcasted_iota(jnp.int32, (tq, tk), 0) + qi * tq
                col = lax.broadcasted_iota(jnp.int32, (tq, tk), 1) + k0
                s = jnp.where(col <= row, s, NEG_INF)
            return _softmax_step(s, v, carry)

        n_full, n_end = _causal_blocks(qi, tq, tk)
        carry = (jnp.full((tq, 1), NEG_INF, F32), jnp.zeros((tq, 1), F32), jnp.zeros((tq, LANES), F32))
        for kd in range(n_full, n_end):
            carry = step(kd, carry, True)
        for ki in range(n_full):
            carry = step(ki, carry, False)
        _, l, acc = carry
        o_ref[pl.ds(qi * tq, tq), :] = (acc / l).astype(o_ref.dtype)


def _kv_spec(kv, layer, seq):
    if kv.ndim == 4:
        return pl.BlockSpec((None, None, seq, LANES), lambda b, h: (b, h, 0, 0))
    return pl.BlockSpec((None, None, None, seq, LANES), lambda b, h: (layer, b, h, 0, 0))


def _fox_prompt(q, k, v, layer, e_rows, tq, tk):
    nb, nh, seq, _ = q.shape
    hm = lambda b, h: (b, h, 0, 0)
    return pl.pallas_call(
        functools.partial(_fox_prompt_kernel, seq=seq, tq=tq, tk=tk),
        grid=(nb, nh),
        in_specs=[pl.BlockSpec((None, None, seq, LANES), hm),
                  _kv_spec(k, layer, seq),
                  _kv_spec(v, layer, seq),
                  pl.BlockSpec((None, None, 1, seq), hm)],
        out_specs=pl.BlockSpec((seq, LANES), lambda b, h: (b, h)),
        out_shape=jax.ShapeDtypeStruct((nb * seq, nh * LANES), BF16),
        compiler_params=_cparams(2, 32),
        name="fox_prompt_attn",
    )(q, k, v, e_rows)


def _diff_lambda(lq1, lk1, lq2, lk2, lam_init):
    return (jnp.exp(jnp.sum(lq1[...] * lk1[...], axis=-1, keepdims=True))
            - jnp.exp(jnp.sum(lq2[...] * lk2[...], axis=-1, keepdims=True)) + lam_init)


def _diff_finish(acc0, l0, acc1, l1, lam, gain, lam_init):
    o = acc0 / l0 - lam * (acc1 / l1)
    o = o * lax.rsqrt(jnp.mean(o * o, axis=-1, keepdims=True) + NORM_EPS) * gain
    return o * (1.0 - lam_init)


def _diff_prompt_kernel(q_ref, k_ref, v_ref, lq1, lk1, lq2, lk2, gs_ref, o_ref, *, seq, tq, tk, lam_init, n_heads):
    h = pl.program_id(1)
    slope = jnp.exp2(-(8.0 / n_heads) * (h + 1).astype(F32) * jnp.ones((1, tk), F32))
    lam = _diff_lambda(lq1, lk1, lq2, lk2, lam_init)
    lo = lax.broadcasted_iota(jnp.int32, (1, LANES), 1) < (LANES // 2)
    for qi in range(seq // tq):
        q = q_ref[pl.ds(qi * tq, tq), :]
        zero = jnp.zeros_like(q)
        q2 = jnp.concatenate([jnp.where(lo, q, zero), jnp.where(lo, zero, q)], axis=0)

        def step(ki, carry, masked, q2=q2, qi=qi):
            k0 = ki * tk
            k = k_ref[pl.ds(k0, tk), :].astype(BF16)
            v = v_ref[pl.ds(k0, tk), :].astype(BF16)
            kpos = (lax.broadcasted_iota(jnp.int32, (1, tk), 1) + (k0 - qi * tq)).astype(F32)
            s = lax.dot_general(q2, k, NT_DIMS, preferred_element_type=F32) + slope * kpos
            if masked:
                row = lax.broadcasted_iota(jnp.int32, (2 * tq, tk), 0)
                row = jnp.where(row >= tq, row - tq, row) + qi * tq
                col = lax.broadcasted_iota(jnp.int32, (2 * tq, tk), 1) + k0
                s = jnp.where(col <= row, s, NEG_INF)
            return _softmax_step(s, v, carry)

        n_full, n_end = _causal_blocks(qi, tq, tk)
        carry = (jnp.full((2 * tq, 1), NEG_INF, F32), jnp.zeros((2 * tq, 1), F32), jnp.zeros((2 * tq, LANES), F32))
        for kd in range(n_full, n_end):
            carry = step(kd, carry, True)
        for ki in range(n_full):
            carry = step(ki, carry, False)
        _, l, acc = carry
        out = _diff_finish(acc[:tq], l[:tq], acc[tq:], l[tq:], lam, gs_ref[...], lam_init)
        o_ref[pl.ds(qi * tq, tq), :] = out.astype(o_ref.dtype)


def _diff_prompt(q, k, v, lams, g_sub, nb, seq, tq, tk, lam_init):
    nh = q.shape[1] // LANES
    blk = pl.BlockSpec((seq, LANES), lambda b, h: (b, h))
    vec64 = pl.BlockSpec((1, LANES // 2), lambda b, h: (0, 0))
    return pl.pallas_call(
        functools.partial(_diff_prompt_kernel, seq=seq, tq=tq, tk=tk, lam_init=lam_init, n_heads=nh),
        grid=(nb, nh),
        in_specs=[blk, blk, blk, vec64, vec64, vec64, vec64, pl.BlockSpec((1, LANES), lambda b, h: (0, 0))],
        out_specs=blk,
        out_shape=jax.ShapeDtypeStruct(q.shape, BF16),
        compiler_params=_cparams(2, 32),
        name="diff_prompt_attn",
    )(q, k, v, *lams, g_sub)


def _sb_block(z, v, later, acc, u, valid):
    l1mb = -jnp.maximum(z, 0.0) - jnp.log(1.0 + jnp.exp(-jnp.abs(z)))
    if valid is not None:
        l1mb = jnp.where(valid, l1mb, 0.0)
    after = _suffix_sum(l1mb, u) + later
    a = jnp.exp(l1mb + z + after)
    if valid is not None:
        a = jnp.where(valid, a, 0.0)
    acc = acc + jnp.dot(a.astype(BF16), v, preferred_element_type=F32)
    return later + jnp.sum(l1mb, axis=-1, keepdims=True), acc


def _sb_prompt_kernel(q_ref, k_ref, v_ref, o_ref, *, seq, tq, tk):
    u = _suffix_ones(tk)
    for qi in range(seq // tq):
        q = q_ref[pl.ds(qi * tq, tq), :]

        def step(ki, carry, masked, q=q, qi=qi):
            k0 = ki * tk
            k = k_ref[pl.ds(k0, tk), :].astype(BF16)
            v = v_ref[pl.ds(k0, tk), :].astype(BF16)
            z = lax.dot_general(q, k, NT_DIMS, preferred_element_type=F32)
            valid = None
            if masked:
                row = lax.broadcasted_iota(jnp.int32, (tq, tk), 0) + qi * tq
                col = lax.broadcasted_iota(jnp.int32, (tq, tk), 1) + k0
                valid = col < row
            return _sb_block(z, v, carry[0], carry[1], u, valid)

        n_full, n_end = _causal_blocks(qi, tq, tk)
        carry = (jnp.zeros((tq, 1), F32), jnp.zeros((tq, LANES), F32))
        for kd in reversed(range(n_full, n_end)):
            carry = step(kd, carry, True)
        for ki in reversed(range(n_full)):
            carry = step(ki, carry, False)
        o_ref[pl.ds(qi * tq, tq), :] = carry[1].astype(o_ref.dtype)


def _sb_prompt(q, k, v, layer, tq, tk):
    nb, nh, seq, _ = q.shape
    hm = lambda b, h: (b, h, 0, 0)
    return pl.pallas_call(
        functools.partial(_sb_prompt_kernel, seq=seq, tq=tq, tk=tk),
        grid=(nb, nh),
        in_specs=[pl.BlockSpec((None, None, seq, LANES), hm), _kv_spec(k, layer, seq), _kv_spec(v, layer, seq)],
        out_specs=pl.BlockSpec((seq, LANES), lambda b, h: (b, h)),
        out_shape=jax.ShapeDtypeStruct((nb * seq, nh * LANES), BF16),
        compiler_params=_cparams(2, 32),
        name="sb_prompt_attn",
    )(q, k, v)


def _page_specs(cache_block, layer, n_per_step, page_of):
    def spec(p):
        return pl.BlockSpec((None, None) + cache_block,
                            lambda b, j, pt: (layer, pt[b, page_of(j, p)]) + (0,) * len(cache_block))
    return [spec(p) for p in range(n_per_step)]


def _new_token_mask(rows):
    t = jnp.bitwise_and(lax.broadcasted_iota(jnp.int32, (rows, LANES), 0), SUBLANES - 1)
    return lax.broadcasted_iota(jnp.int32, (rows, LANES), 1), t


def _fox_dec_kernel(pt_ref, q_ref, e_ref, en_ref, kn_ref, vn_ref, *rest, n_slots, n_heads, n_steps):
    k_refs, v_refs = rest[:n_slots], rest[n_slots:2 * n_slots]
    o_ref, m_sc, l_sc, acc_sc = rest[2 * n_slots:]
    j = pl.program_id(1)
    rows = n_heads * SUBLANES

    @pl.when(j == 0)
    def _():
        m_sc[...] = jnp.full((rows, 1), NEG_INF, F32)
        l_sc[...] = jnp.zeros((rows, 1), F32)
        acc_sc[...] = jnp.zeros((rows, LANES), F32)

    def scores(k_of, e_of):
        return jnp.concatenate(
            [lax.dot_general(q_ref[h], k_of(h).astype(BF16), NT_DIMS, preferred_element_type=F32) + e_of(h)
             for h in range(n_heads)], axis=0)

    def update(s, v_of):
        m_prev = m_sc[...]
        m_new = jnp.maximum(m_prev, jnp.max(s, axis=-1, keepdims=True))
        alpha = jnp.exp(m_prev - m_new)
        p = jnp.exp(s - m_new)
        l_sc[...] = alpha * l_sc[...] + jnp.sum(p, axis=-1, keepdims=True)
        pb = p.astype(BF16)
        pv = []
        for h in range(n_heads):
            tot = None
            for pg in range(s.shape[1] // LANES):
                part = jnp.dot(pb[h * SUBLANES:(h + 1) * SUBLANES, pg * LANES:(pg + 1) * LANES],
                               v_of(pg, h).astype(BF16), preferred_element_type=F32)
                tot = part if tot is None else tot + part
            pv.append(tot)
        acc_sc[...] = alpha * acc_sc[...] + jnp.concatenate(pv, axis=0)
        m_sc[...] = m_new

    s_past = jnp.concatenate(
        [scores(lambda h, p=p: k_refs[p][h], lambda h, p=p: e_ref[p, h:h + 1, :]) for p in range(n_slots)], axis=1)
    update(s_past, lambda pg, h: v_refs[pg][h])

    @pl.when(j == n_steps - 1)
    def _():
        col, t = _new_token_mask(rows)
        s_new = scores(lambda h: kn_ref[h], lambda h: en_ref[h:h + 1, :])
        update(jnp.where(col <= t, s_new, NEG_INF), lambda pg, h: vn_ref[h])
        o_ref[...] = (acc_sc[...] / l_sc[...]).reshape(n_heads, SUBLANES, LANES).astype(o_ref.dtype)


def _fox_decode(page_table, q, e_all, k_new, v_new, cache_k, cache_v, layer, n_slots):
    nb, n_pages = page_table.shape
    n_heads = q.shape[1]
    n_steps = n_pages // n_slots
    page_blk = (n_heads, LANES, LANES)
    page_of = lambda j, p: j * n_slots + p
    per_b = lambda b, j, pt: (b, 0, 0, 0)
    grid_spec = pltpu.PrefetchScalarGridSpec(
        num_scalar_prefetch=1,
        grid=(nb, n_steps),
        in_specs=[pl.BlockSpec((None, n_heads, SUBLANES, LANES), per_b),
                  pl.BlockSpec((None, n_slots, HEAD_PAD, LANES), lambda b, j, pt: (b, j, 0, 0)),
                  pl.BlockSpec((None, None, HEAD_PAD, LANES), lambda b, j, pt: (b, n_pages, 0, 0)),
                  pl.BlockSpec((None,) + page_blk, per_b),
                  pl.BlockSpec((None,) + page_blk, per_b)]
                 + _page_specs(page_blk, layer, n_slots, page_of) + _page_specs(page_blk, layer, n_slots, page_of),
        out_specs=pl.BlockSpec((None, n_heads, SUBLANES, LANES), per_b),
        scratch_shapes=[pltpu.VMEM((n_heads * SUBLANES, 1), F32), pltpu.VMEM((n_heads * SUBLANES, 1), F32),
                        pltpu.VMEM((n_heads * SUBLANES, LANES), F32)])
    return pl.pallas_call(
        functools.partial(_fox_dec_kernel, n_slots=n_slots, n_heads=n_heads, n_steps=n_steps),
        grid_spec=grid_spec,
        out_shape=jax.ShapeDtypeStruct((nb, n_heads, SUBLANES, LANES), BF16),
        compiler_params=_cparams(2, 48),
        name="fox_decode_attn",
    )(page_table, q, e_all, e_all, k_new, v_new, *([cache_k] * n_slots), *([cache_v] * n_slots))


def _diff_dec_kernel(pt_ref, q_ref, kn_ref, vn_ref, lq1, lk1, lq2, lk2, gs_ref, *rest,
                     n_slots, n_heads, n_steps, past, lam_init):
    k_refs, v_refs = rest[:n_slots], rest[n_slots:2 * n_slots]
    o_ref, m_sc, l_sc, acc_sc = rest[2 * n_slots:]
    j = pl.program_id(1)
    hr = 2 * SUBLANES
    rows = n_heads * hr
    lo = lax.broadcasted_iota(jnp.int32, (1, LANES), 1) < (LANES // 2)

    @pl.when(j == 0)
    def _():
        m_sc[...] = jnp.full((rows, 1), NEG_INF, F32)
        l_sc[...] = jnp.zeros((rows, 1), F32)
        acc_sc[...] = jnp.zeros((rows, LANES), F32)

    col, t = _new_token_mask(hr)

    def scores(kt_of, first_pos):
        dist = ((past + t) - (first_pos + col)).astype(F32)
        out = []
        for h in range(n_heads):
            q = q_ref[h]
            zero = jnp.zeros_like(q)
            q2 = jnp.concatenate([jnp.where(lo, q, zero), jnp.where(lo, zero, q)], axis=0)
            s = jnp.dot(q2, kt_of(h).astype(BF16), preferred_element_type=F32)
            out.append(s - (2.0 ** (-(8.0 / n_heads) * (h + 1))) * dist)
        return jnp.concatenate(out, axis=0)

    ncol = LANES * n_heads
    slot_of_col, _ = _div_mod(lax.broadcasted_iota(jnp.int32, (LANES, ncol), 1), n_heads)
    spread = jnp.where(slot_of_col == lax.broadcasted_iota(jnp.int32, (LANES, ncol), 0), 1.0, 0.0).astype(BF16)
    head_of_row, _ = _div_mod(lax.broadcasted_iota(jnp.int32, (rows, ncol), 0), hr)
    own_head = head_of_row == _div_mod(lax.broadcasted_iota(jnp.int32, (rows, ncol), 1), n_heads)[1]

    def update(s, v_of):
        m_prev = m_sc[...]
        m_new = jnp.maximum(m_prev, jnp.max(s, axis=-1, keepdims=True))
        alpha = jnp.exp(m_prev - m_new)
        p = jnp.exp(s - m_new)
        l_sc[...] = alpha * l_sc[...] + jnp.sum(p, axis=-1, keepdims=True)
        pb = p.astype(BF16)
        tot = None
        for pg in range(s.shape[1] // LANES):
            wide = jnp.dot(pb[:, pg * LANES:(pg + 1) * LANES], spread, preferred_element_type=F32)
            wide = jnp.where(own_head, wide, 0.0).astype(BF16)
            part = jnp.dot(wide, v_of(pg).astype(BF16), preferred_element_type=F32)
            tot = part if tot is None else tot + part
        acc_sc[...] = alpha * acc_sc[...] + tot
        m_sc[...] = m_new

    s_past = jnp.concatenate(
        [scores(lambda h, p=p: k_refs[p][h], (j * n_slots + p) * LANES) for p in range(n_slots)], axis=1)
    update(s_past, lambda pg: v_refs[pg][...].reshape(ncol, LANES))

    @pl.when(j == n_steps - 1)
    def _():
        s_new = scores(lambda h: kn_ref[h], past)
        valid = jnp.concatenate([col <= t] * n_heads, axis=0)
        update(jnp.where(valid, s_new, NEG_INF), lambda pg: vn_ref[...].reshape(ncol, LANES))
        lam = _diff_lambda(lq1, lk1, lq2, lk2, lam_init)
        acc, l = acc_sc[...], l_sc[...]
        for h in range(n_heads):
            r0 = h * hr
            out = _diff_finish(acc[r0:r0 + SUBLANES], l[r0:r0 + SUBLANES], acc[r0 + SUBLANES:r0 + hr],
                               l[r0 + SUBLANES:r0 + hr], lam, gs_ref[...], lam_init)
            o_ref[h] = out.astype(o_ref.dtype)


def _diff_decode(page_table, q, kt_new, v_new, lams, g_sub, cache_kt, cache_v, layer, n_slots, lam_init):
    nb, n_pages = page_table.shape
    n_heads = q.shape[1]
    n_steps = n_pages // n_slots
    kt_blk, v_blk = (n_heads, LANES, LANES), (LANES, n_heads, LANES)
    page_of = lambda j, p: j * n_slots + p
    per_b = lambda b, j, pt: (b, 0, 0, 0)
    vec64 = pl.BlockSpec((1, LANES // 2), lambda b, j, pt: (0, 0))
    rows = n_heads * 2 * SUBLANES
    grid_spec = pltpu.PrefetchScalarGridSpec(
        num_scalar_prefetch=1,
        grid=(nb, n_steps),
        in_specs=[pl.BlockSpec((None, n_heads, SUBLANES, LANES), per_b),
                  pl.BlockSpec((None,) + kt_blk, per_b),
                  pl.BlockSpec((None,) + v_blk, per_b),
                  vec64, vec64, vec64, vec64, pl.BlockSpec((1, LANES), lambda b, j, pt: (0, 0))]
                 + _page_specs(kt_blk, layer, n_slots, page_of) + _page_specs(v_blk, layer, n_slots, page_of),
        out_specs=pl.BlockSpec((None, n_heads, SUBLANES, LANES), per_b),
        scratch_shapes=[pltpu.VMEM((rows, 1), F32), pltpu.VMEM((rows, 1), F32), pltpu.VMEM((rows, LANES), F32)])
    return pl.pallas_call(
        functools.partial(_diff_dec_kernel, n_slots=n_slots, n_heads=n_heads, n_steps=n_steps,
                          past=n_pages * LANES, lam_init=lam_init),
        grid_spec=grid_spec,
        out_shape=jax.ShapeDtypeStruct((nb, n_heads, SUBLANES, LANES), BF16),
        compiler_params=_cparams(2, 48),
        name="diff_decode_attn",
    )(page_table, q, kt_new, v_new, *lams, g_sub, *([cache_kt] * n_slots), *([cache_v] * n_slots))


def _sb_dec_kernel(pt_ref, q_ref, kn_ref, vn_ref, *rest, n_slots, n_heads, n_steps):
    k_refs, v_refs = rest[:n_slots], rest[n_slots:2 * n_slots]
    o_ref, later_sc, acc_sc = rest[2 * n_slots:]
    j = pl.program_id(1)
    rows = n_heads * SUBLANES
    u = _suffix_ones(LANES)

    def pages(k_of, v_of, n, valid):
        z = jnp.concatenate(
            [lax.dot_general(q_ref[h], k_of(pg, h).astype(BF16), NT_DIMS, preferred_element_type=F32)
             for pg in range(n) for h in range(n_heads)], axis=0)
        l1mb = -jnp.maximum(z, 0.0) - jnp.log(1.0 + jnp.exp(-jnp.abs(z)))
        if valid is not None:
            l1mb = jnp.where(valid, l1mb, 0.0)
        page_sum = jnp.sum(l1mb, axis=-1, keepdims=True)
        later = later_sc[...]
        laters = []
        for pg in range(n):
            laters.append(later)
            later = later + page_sum[pg * rows:(pg + 1) * rows]
        after = _suffix_sum(l1mb, u) + jnp.concatenate(laters, axis=0)
        a = jnp.exp(l1mb + z + after)
        if valid is not None:
            a = jnp.where(valid, a, 0.0)
        ab = a.astype(BF16)
        pv = []
        for h in range(n_heads):
            tot = None
            for pg in range(n):
                r0 = pg * rows + h * SUBLANES
                part = jnp.dot(ab[r0:r0 + SUBLANES], v_of(pg, h).astype(BF16), preferred_element_type=F32)
                tot = part if tot is None else tot + part
            pv.append(tot)
        acc_sc[...] += jnp.concatenate(pv, axis=0)
        later_sc[...] = later

    @pl.when(j == 0)
    def _():
        later_sc[...] = jnp.zeros((rows, 1), F32)
        acc_sc[...] = jnp.zeros((rows, LANES), F32)
        col, t = _new_token_mask(rows)
        pages(lambda pg, h: kn_ref[h], lambda pg, h: vn_ref[h], 1, col < t)

    pages(lambda pg, h: k_refs[pg][h], lambda pg, h: v_refs[pg][h], n_slots, None)

    @pl.when(j == n_steps - 1)
    def _():
        o_ref[...] = acc_sc[...].reshape(n_heads, SUBLANES, LANES).astype(o_ref.dtype)


def _sb_decode(page_table, q, k_new, v_new, cache_k, cache_v, layer, n_slots):
    nb, n_pages = page_table.shape
    n_heads = q.shape[1]
    n_steps = n_pages // n_slots
    page_blk = (n_heads, LANES, LANES)
    page_of = lambda j, p: n_pages - 1 - (j * n_slots + p)
    per_b = lambda b, j, pt: (b, 0, 0, 0)
    grid_spec = pltpu.PrefetchScalarGridSpec(
        num_scalar_prefetch=1,
        grid=(nb, n_steps),
        in_specs=[pl.BlockSpec((None, n_heads, SUBLANES, LANES), per_b),
                  pl.BlockSpec((None,) + page_blk, per_b),
                  pl.BlockSpec((None,) + page_blk, per_b)]
                 + _page_specs(page_blk, layer, n_slots, page_of) + _page_specs(page_blk, layer, n_slots, page_of),
        out_specs=pl.BlockSpec((None, n_heads, SUBLANES, LANES), per_b),
        scratch_shapes=[pltpu.VMEM((n_heads * SUBLANES, 1), F32), pltpu.VMEM((n_heads * SUBLANES, LANES), F32)])
    return pl.pallas_call(
        functools.partial(_sb_dec_kernel, n_slots=n_slots, n_heads=n_heads, n_steps=n_steps),
        grid_spec=grid_spec,
        out_shape=jax.ShapeDtypeStruct((nb, n_heads, SUBLANES, LANES), BF16),
        compiler_params=_cparams(2, 48),
        name="sb_decode_attn",
    )(page_table, q, k_new, v_new, *([cache_k] * n_slots), *([cache_v] * n_slots))


def _merge_kernel(oa, ob, oc, wa, wb, wc, g0, g1, g2, o_ref):
    pa = jnp.dot(oa[...], wa[...].astype(BF16), preferred_element_type=F32)
    pb = jnp.dot(ob[...], wb[...].astype(BF16), preferred_element_type=F32)
    pc = jnp.dot(oc[...], wc[...].astype(BF16), preferred_element_type=F32)
    o_ref[...] = (g0[...] * pa + g1[...] * pb + g2[...] * pc).astype(o_ref.dtype)


def _merge(oa, ob, oc, wa, wb, wc, layer, gates, tm, tn):
    m = oa.shape[0]
    d_model = wa.shape[2]
    nj = d_model // tn
    a_spec = lambda a: pl.BlockSpec((tm, a.shape[1]), lambda i, j: (i, 0))
    w_spec = lambda w: pl.BlockSpec((None, w.shape[1], tn), lambda i, j: (layer, 0, j))
    g_spec = lambda c: pl.BlockSpec((tm, tn), lambda i, j: (i, c * nj + j))
    return pl.pallas_call(
        _merge_kernel,
        grid=(m // tm, nj),
        in_specs=[a_spec(oa), a_spec(ob), a_spec(oc), w_spec(wa), w_spec(wb), w_spec(wc),
                  g_spec(0), g_spec(1), g_spec(2)],
        out_specs=pl.BlockSpec((tm, tn), lambda i, j: (i, j)),
        out_shape=jax.ShapeDtypeStruct((m, d_model), BF16),
        compiler_params=_cparams(2, 48),
        name="branch_merge",
    )(oa, ob, oc, wa, wb, wc, gates, gates, gates)


def _project(h, g, w_t, layer, name, col0, ncols, tn, epilogue, extras=(), extra_specs=(), heads=None, dtype=F32,
             stack=None, into=None):
    m, d_model = h.shape
    if heads is not None:
        shape, spec = _heads_out(g, heads, tn // LANES, dtype, stack)
    else:
        shape = jax.ShapeDtypeStruct((m, ncols), dtype)
        spec = pl.BlockSpec((g.tm, tn), lambda i, j, k: (i, j))
    return _mm(name, h, w_t, layer, col0, ncols, g.tm, tn, d_model, epilogue, list(extras), list(extra_specs),
               shape, spec, w_rows_are_outputs=True, into=into)


def _vec_spec(n):
    return pl.BlockSpec((1, n), lambda i, j, k: (0, 0))


def _layer_front(x, mod, g, p, dims, stack=None, prev=None):
    fox_w, dqk_w, dv_w, sb_w, d_model = dims
    kv = lambda name: dict(stack=stack, into=None if prev is None else prev[name])
    fh, sh = fox_w // LANES, sb_w // LANES
    h = _prenorm(x, p["g_norm1"], mod, g, 1, 0)
    tn = 512
    sc128 = 1.0 / math.sqrt(LANES)
    rms = lambda scale: functools.partial(_epi_heads_rms, g=g, scale=scale)
    plain = lambda scale: functools.partial(_epi_heads_plain, g=g, scale=scale)
    half = lambda scale: functools.partial(_epi_rms_half, scale=scale)
    out = {}
    proj = functools.partial(_project, h, g, p["w_in_t"], p["layer"])
    c = 0
    out["qa"] = proj("proj_fox_q", c, fox_w, tn, rms(sc128), [p["g_q_fox"]], [_vec_spec(LANES)], heads=fh, dtype=BF16); c += fox_w
    out["ka"] = proj("proj_fox_k", c, fox_w, tn, rms(1.0), [p["g_k_fox"]], [_vec_spec(LANES)], heads=fh, **kv("ka")); c += fox_w
    out["va"] = proj("proj_fox_v", c, fox_w, tn, plain(1.0), heads=fh, **kv("va")); c += fox_w
    out["qb"] = proj("proj_diff_q", c, dqk_w, tn, half(0.125), [p["g_q_diff"]], [_vec_spec(LANES)], dtype=BF16); c += dqk_w
    out["kb"] = proj("proj_diff_k", c, dqk_w, tn, half(1.0), [p["g_k_diff"]], [_vec_spec(LANES)]); c += dqk_w
    out["vb"] = proj("proj_diff_v", c, dv_w, tn, _epi_plain2d); c += dv_w
    out["qc"] = proj("proj_sb_q", c, sb_w, tn, plain(sc128), heads=sh, dtype=BF16); c += sb_w
    out["kc"] = proj("proj_sb_k", c, sb_w, tn, plain(1.0), heads=sh, **kv("kc")); c += sb_w
    out["vc"] = proj("proj_sb_v", c, sb_w, tn, plain(1.0), heads=sh, **kv("vc")); c += sb_w
    out["gates"] = proj("proj_gates", c, 3 * d_model, tn, _epi_sigmoid, dtype=BF16)
    out["lf_t"] = _lf_proj(p["w_fa_t"], h, p["b_fa"])
    return out


def _layer_back(x, mod, g, p, oa, ob, oc, gates):
    m, d_model = x.shape
    layer = p["layer"]
    merged = _merge(oa, ob, oc, p["w_br_fox"], p["w_br_diff"], p["w_br_sb"], layer, gates, g.tm, 256)
    tn = 512
    xspec = pl.BlockSpec((g.tm, tn), lambda i, j, k: (i, j))
    oshape = jax.ShapeDtypeStruct((m, d_model), F32)
    x1 = _mm("out_proj", merged, p["w_out"], layer, 0, d_model, g.tm, tn, d_model, _epi_residual,
             [x, mod], [xspec, _mod_spec(g, 2, d_model, tn)], oshape, xspec)
    h2 = _prenorm(x1, p["g_norm2"], mod, g, 4, 3)
    d_ff = p["w_up"].shape[2]
    f = _mm("mlp_up", h2, p["w_up"], layer, 0, d_ff, g.tm, tn, d_model, _epi_relu2, [], [],
            jax.ShapeDtypeStruct((m, d_ff), BF16), xspec)
    tn2 = _pick(d_model, 1024)
    xspec2 = pl.BlockSpec((g.tm, tn2), lambda i, j, k: (i, j))
    return _mm("mlp_down", f, p["w_down"], layer, 0, d_model, g.tm, tn2, _pick(d_ff, 2048), _epi_residual,
               [x1, mod], [xspec2, _mod_spec(g, 5, d_model, tn2)], oshape, xspec2)


def kernel(x_prompt, x_sample, cache_fox_k, cache_fox_v, cache_fox_logf, cache_diff_k, cache_diff_v, cache_sb_k, cache_sb_v, page_table, c_prompt, c_sample, w_ada, b_ada, g_norm1, w_in, b_fgate, g_q_fox, g_k_fox, g_q_diff, g_k_diff, lambda_q1, lambda_k1, lambda_q2, lambda_k2, g_sub_diff, w_br_fox, w_br_diff, w_br_sb, w_out, g_norm2, w_up, w_down):
    n_layers = w_ada.shape[0]
    nbp, seq, d_model = x_prompt.shape
    nbs, dseq, _ = x_sample.shape
    assert dseq == SUBLANES, "decode kernels hold the new tokens of a sequence in one sublane group"
    n_pages = page_table.shape[1]
    n_pool, page_size, fh, fd = cache_fox_k.shape[1:]
    dh, dqk = cache_diff_k.shape[3], cache_diff_k.shape[5]
    sh = cache_sb_k.shape[3]
    assert page_size == LANES and fd == LANES and 2 * dqk == LANES and cache_sb_k.shape[4] == LANES
    fox_w, dqk_w, dv_w, sb_w = fh * LANES, dh * LANES, dh * LANES, sh * LANES
    dims = (fox_w, dqk_w, dv_w, sb_w, d_model)
    past = n_pages * page_size
    mp, ms = nbp * seq, nbs * dseq

    gp = Group(nbp, seq, _pick(seq, 1024), False)
    gs = Group(nbs, dseq, ms, True)
    tq_fox, tq_diff, tq_sb = _pick(seq, 1024), _pick(seq, 512), _pick(seq, 512)
    tk_fox, tk_diff, tk_sb = _pick(seq, 512), _pick(seq, 512), _pick(seq, 256)
    n_slots = max(s for s in (8, 4, 2, 1) if n_pages % s == 0)

    n_c = nbp + nbs
    c_rows = -(-n_c // SUBLANES) * SUBLANES
    c_all = jnp.concatenate([c_prompt, c_sample, jnp.zeros((c_rows - n_c, d_model), F32)], axis=0)
    mod_all = _ada(c_all, w_ada, b_ada)

    fox_k_hm = jnp.transpose(cache_fox_k, (0, 1, 3, 2, 4))
    fox_v_hm = jnp.transpose(cache_fox_v, (0, 1, 3, 2, 4))
    sb_k_hm = jnp.transpose(cache_sb_k, (0, 1, 3, 2, 4))
    sb_v_hm = jnp.transpose(cache_sb_v, (0, 1, 3, 2, 4))
    diff_kt = jnp.transpose(cache_diff_k, (0, 1, 3, 4, 5, 2)).reshape(n_layers, n_pool, dh, LANES, page_size)
    logf_t = jnp.transpose(cache_fox_logf, (0, 3, 1, 2))
    pool_pad = -(-n_pool // SUBLANES) * SUBLANES

    c_fa = 3 * fox_w
    n_fa = fh
    yp, ys = x_prompt.reshape(mp, d_model), x_sample.reshape(ms, d_model)
    outs_p, outs_s = [], []
    w_in_t = jnp.transpose(w_in, (2, 0, 1))
    w_in_l = _w_in_by_layer(w_in, c_fa, n_fa, 256)
    w_down_bf = w_down.astype(BF16)
    for l in range(n_layers):
        lam_init = 0.8 - 0.6 * math.exp(-0.3 * l)
        p = {
            "layer": l,
            "w_in_t": w_in_l,
            "w_fa_t": jnp.pad(w_in_t[c_fa:c_fa + n_fa, l], ((0, HEAD_PAD - n_fa), (0, 0))).astype(BF16),
            "b_fa": jnp.pad(b_fgate[l], (0, HEAD_PAD - n_fa)).reshape(HEAD_PAD, 1),
            "g_norm1": g_norm1[l], "g_norm2": g_norm2[l],
            "g_q_fox": g_q_fox[l].reshape(1, LANES), "g_k_fox": g_k_fox[l].reshape(1, LANES),
            "g_q_diff": jnp.tile(g_q_diff[l], 2).reshape(1, LANES), "g_k_diff": jnp.tile(g_k_diff[l], 2).reshape(1, LANES),
            "w_br_fox": w_br_fox, "w_br_diff": w_br_diff, "w_br_sb": w_br_sb,
            "w_out": w_out, "w_up": w_up, "w_down": w_down_bf,
        }
        lams = [v[l].reshape(1, dqk) for v in (lambda_q1, lambda_k1, lambda_q2, lambda_k2)]
        g_sub = g_sub_diff[l].reshape(1, LANES)

        mod_p = mod_all[l, :nbp].reshape(nbp, 1, -1)
        fr = _layer_front(yp, mod_p, gp, p, dims, stack=(n_layers, l), prev=fr if l else None)
        lf_p = fr["lf_t"]
        x_e = lf_p.reshape(HEAD_PAD, nbp, seq // LANES, LANES).transpose(1, 2, 0, 3)
        e_p = _ecum(x_e).transpose(0, 2, 1, 3).reshape(nbp, HEAD_PAD, 1, seq)
        oa = _fox_prompt(fr["qa"], fr["ka"], fr["va"], l, e_p, tq_fox, tk_fox)
        ob = _diff_prompt(fr["qb"], fr["kb"], fr["vb"], lams, g_sub, nbp, seq, tq_diff, tk_diff, lam_init)
        oc = _sb_prompt(fr["qc"], fr["kc"], fr["vc"], l, tq_sb, tk_sb)
        yp = _layer_back(yp, mod_p, gp, p, oa, ob, oc, fr["gates"])
        hm_out = lambda a: jnp.transpose(a, (0, 2, 1, 3))
        outs_p.append((
            lf_p[:fh].reshape(fh, nbp, seq).transpose(1, 2, 0),
            fr["kb"].reshape(nbp, seq, dh, 2, dqk), fr["vb"].reshape(nbp, seq, dh, LANES)))

        mod_s = jnp.repeat(mod_all[l, nbp:nbp + nbs], dseq, axis=0)
        fs = _layer_front(ys, mod_s, gs, p, dims)
        lf_s = fs["lf_t"]
        lf_new = jnp.pad(lf_s.reshape(HEAD_PAD, nbs, dseq).transpose(1, 0, 2), ((0, 0), (0, 0), (0, LANES - dseq)))
        table = jnp.pad(logf_t[l], ((0, 0), (0, pool_pad - n_pool), (0, 0))).reshape(fh, pool_pad // SUBLANES, SUBLANES, LANES)
        e_s = _ecum(_lf_gather(page_table, table, lf_new))
        pad_keys = lambda a: jnp.pad(a, ((0, 0), (0, 0), (0, LANES - dseq), (0, 0)))
        oa = _fox_decode(page_table, fs["qa"], e_s, pad_keys(fs["ka"]), pad_keys(fs["va"]), fox_k_hm, fox_v_hm, l, n_slots)
        qb_hm = fs["qb"].reshape(nbs, dseq, dh, LANES).transpose(0, 2, 1, 3)
        kb_t = fs["kb"].reshape(nbs, dseq, dh, LANES).transpose(0, 2, 3, 1)
        kt_new = jnp.pad(kb_t, ((0, 0), (0, 0), (0, 0), (0, LANES - dseq)))
        vb_new = jnp.pad(fs["vb"].reshape(nbs, dseq, dh, LANES), ((0, 0), (0, LANES - dseq), (0, 0), (0, 0)))
        ob = _diff_decode(page_table, qb_hm, kt_new, vb_new, lams, g_sub, diff_kt, cache_diff_v, l, n_slots, lam_init)
        oc = _sb_decode(page_table, fs["qc"], pad_keys(fs["kc"]), pad_keys(fs["vc"]), sb_k_hm, sb_v_hm, l, n_slots)
        rows2d = lambda a: jnp.transpose(a, (0, 2, 1, 3)).reshape(ms, -1)
        ys = _layer_back(ys, mod_s, gs, p, rows2d(oa), rows2d(ob), rows2d(oc), fs["gates"])
        outs_s.append((
            hm_out(fs["ka"]), hm_out(fs["va"]), lf_s[:fh].reshape(fh, nbs, dseq).transpose(1, 2, 0),
            fs["kb"].reshape(nbs, dseq, dh, 2, dqk), fs["vb"].reshape(nbs, dseq, dh, LANES),
            hm_out(fs["kc"]), hm_out(fs["vc"])))

    stack = lambda outs: tuple(jnp.stack([o[i] for o in outs]) for i in range(len(outs[0])))
    p_lf, p_dk, p_dv = stack(outs_p)
    hm_stacked = lambda a: jnp.transpose(a, (0, 1, 3, 2, 4))
    prompt_rows = (hm_stacked(fr["ka"]), hm_stacked(fr["va"]), p_lf, p_dk, p_dv, hm_stacked(fr["kc"]), hm_stacked(fr["vc"]))
    return (yp.reshape(nbp, seq, d_model), ys.reshape(nbs, dseq, d_model)) + prompt_rows + stack(outs_s)
```

```python
import functools
import math
from typing import NamedTuple

import jax
import jax.numpy as jnp
from jax import lax
from jax.experimental import pallas as pl
from jax.experimental.pallas import tpu as pltpu

F32 = jnp.float32
BF16 = jnp.bfloat16
NORM_EPS = 1e-6
LANES = 128
SUBLANES = 8
HEAD_PAD = 16
NT_DIMS = (((1,), (1,)), ((), ()))
MIB = 1 << 20
NEG_INF = float("-inf")


class Group(NamedTuple):
    B: int
    T: int
    tm: int
    per_row_mod: bool


def _pick(dim, pref):
    t = min(pref, dim)
    while t >= LANES:
        if dim % t == 0 and t % LANES == 0:
            return t
        t -= LANES
    return dim


def _cparams(n_axes, vmem_mib=None):
    return pltpu.CompilerParams(
        dimension_semantics=("arbitrary",) * n_axes,
        vmem_limit_bytes=None if vmem_mib is None else vmem_mib * MIB)


def _log_sigmoid(x):
    return jnp.minimum(x, 0.0) - jnp.log(1.0 + jnp.exp(-jnp.abs(x)))


def _split3(x):
    hi = x.astype(BF16)
    r1 = x - hi.astype(F32)
    mid = r1.astype(BF16)
    lo = (r1 - mid.astype(F32)).astype(BF16)
    return hi, mid, lo


def _suffix_sum(x, u):
    hi = x.astype(BF16)
    lo = (x - hi.astype(F32)).astype(BF16)
    return jnp.dot(hi, u, preferred_element_type=F32) + jnp.dot(lo, u, preferred_element_type=F32)


def _div_mod(x, n):
    if n & (n - 1) == 0:
        return lax.shift_right_logical(x, n.bit_length() - 1), jnp.bitwise_and(x, n - 1)
    return lax.div(x, n), lax.rem(x, n)


def _suffix_ones(n):
    r = lax.broadcasted_iota(jnp.int32, (n, n), 0)
    c = lax.broadcasted_iota(jnp.int32, (n, n), 1)
    return (r > c).astype(BF16)


def _mod_spec(g, chunk, d_model, tn):
    nj = d_model // tn
    if g.per_row_mod:
        return pl.BlockSpec((g.tm, tn), lambda i, j, *_: (i, chunk * nj + j))
    nt = g.T // g.tm
    return pl.BlockSpec((None, 1, tn), lambda i, j, *_: (i // nt, 0, chunk * nj + j))


def _ada_kernel(c_ref, w_ref, b_ref, o_ref):
    c = c_ref[...]
    a = (c * jax.nn.sigmoid(c)).astype(BF16)
    o_ref[...] = jnp.dot(a, w_ref[...].astype(BF16), preferred_element_type=F32) + b_ref[...]


def _ada(c_all, w_ada, b_ada):
    n_layers, d_model, n = w_ada.shape
    rows = c_all.shape[0]
    tn = _pick(n, 512)
    return pl.pallas_call(
        _ada_kernel,
        grid=(n_layers, n // tn),
        in_specs=[pl.BlockSpec((rows, d_model), lambda l, j: (0, 0)),
                  pl.BlockSpec((None, d_model, tn), lambda l, j: (l, 0, j)),
                  pl.BlockSpec((None, 1, tn), lambda l, j: (l, 0, j))],
        out_specs=pl.BlockSpec((None, rows, tn), lambda l, j: (l, 0, j)),
        out_shape=jax.ShapeDtypeStruct((n_layers, rows, n), F32),
        compiler_params=_cparams(2, 40),
        name="ada_modulation",
    )(c_all, w_ada, b_ada.reshape(n_layers, 1, n))


def _w_in_split_kernel(w_ref, o_ref, *, tn, n_layers, n_chunks):
    per_col = n_chunks * n_layers
    for l in range(n_layers):
        for kc in range(n_chunks):
            rows = w_ref[pl.ds(kc * n_layers + l, tn, stride=per_col), :]
            o_ref[l, :, kc * LANES:(kc + 1) * LANES] = rows.astype(o_ref.dtype)


def _w_in_by_layer(w_in, skip0, n_skip, tn):
    n_layers, kdim, n = w_in.shape
    n_chunks = kdim // LANES
    per_col = n_chunks * n_layers
    rows = w_in.reshape(n_layers, n_chunks, LANES, n).transpose(3, 1, 0, 2).reshape(n * per_col, LANES)
    n_out = n - n_skip
    assert skip0 % tn == 0 and n_out % tn == 0

    def src(i):
        col = i * tn
        return (jnp.where(col >= skip0, col + n_skip, col) * per_col, 0)

    return pl.pallas_call(
        functools.partial(_w_in_split_kernel, tn=tn, n_layers=n_layers, n_chunks=n_chunks),
        grid=(n_out // tn,),
        in_specs=[pl.BlockSpec((pl.Element(tn * per_col), pl.Element(LANES)), src)],
        out_specs=pl.BlockSpec((n_layers, tn, kdim), lambda i: (0, i, 0)),
        out_shape=jax.ShapeDtypeStruct((n_layers, n_out, kdim), BF16),
        compiler_params=_cparams(1, 40),
        name="w_in_by_layer",
    )(rows)


def _prenorm_kernel(x_ref, g_ref, sc_ref, sh_ref, o_ref):
    x = x_ref[...]
    ms = jnp.mean(x * x, axis=-1, keepdims=True)
    y = x * lax.rsqrt(ms + NORM_EPS) * g_ref[...]
    o_ref[...] = (y * (1.0 + sc_ref[...]) + sh_ref[...]).astype(o_ref.dtype)


def _prenorm(x, gain, mod, g, scale_chunk, shift_chunk):
    m, d_model = x.shape
    tm = min(g.tm, 256)
    gg = g._replace(tm=tm)
    return pl.pallas_call(
        _prenorm_kernel,
        grid=(m // tm, 1),
        in_specs=[pl.BlockSpec((tm, d_model), lambda i, j: (i, 0)),
                  pl.BlockSpec((1, d_model), lambda i, j: (0, 0)),
                  _mod_spec(gg, scale_chunk, d_model, d_model),
                  _mod_spec(gg, shift_chunk, d_model, d_model)],
        out_specs=pl.BlockSpec((tm, d_model), lambda i, j: (i, 0)),
        out_shape=jax.ShapeDtypeStruct((m, d_model), BF16),
        compiler_params=_cparams(2, 40),
        name="mod_rmsnorm",
    )(x, gain.reshape(1, d_model), mod, mod)


def _mm_kernel(*refs, nk, n_extra, n_aliased, epilogue, w_rows_are_outputs):
    a_ref, w_ref = refs[0], refs[1]
    extras = refs[2:2 + n_extra]
    refs = refs[:2 + n_extra] + refs[2 + n_extra + n_aliased:]
    if w_rows_are_outputs:
        part = lax.dot_general(a_ref[...], w_ref[...].astype(BF16), NT_DIMS, preferred_element_type=F32)
    else:
        part = jnp.dot(a_ref[...], w_ref[...].astype(BF16), preferred_element_type=F32)
    if nk == 1:
        epilogue(part, extras, refs[2 + n_extra:])
        return
    outs, acc_ref = refs[2 + n_extra:-1], refs[-1]
    k = pl.program_id(2)

    @pl.when(k == 0)
    def _():
        acc_ref[...] = part

    @pl.when(k > 0)
    def _():
        acc_ref[...] += part

    @pl.when(k == nk - 1)
    def _():
        epilogue(acc_ref[...], extras, outs)


def _mm(name, a, w, layer, col0, ncols, tm, tn, tk, epilogue, extras, extra_specs, out_shape, out_specs,
        w_rows_are_outputs=False, into=None, vmem_mib=48):
    m, kdim = a.shape
    assert col0 % tn == 0 and ncols % tn == 0 and m % tm == 0 and kdim % tk == 0
    nk, c0 = kdim // tk, col0 // tn
    if w_rows_are_outputs:
        w_spec = pl.BlockSpec((None, tn, tk), lambda i, j, k: (layer, c0 + j, k))
    else:
        w_spec = pl.BlockSpec((None, tk, tn), lambda i, j, k: (layer, k, c0 + j))
    aliased = [] if into is None else [into]
    kern = functools.partial(_mm_kernel, nk=nk, n_extra=len(extras), n_aliased=len(aliased), epilogue=epilogue,
                             w_rows_are_outputs=w_rows_are_outputs)
    return pl.pallas_call(
        kern,
        grid=(m // tm, ncols // tn, nk),
        in_specs=[pl.BlockSpec((tm, tk), lambda i, j, k: (i, k)), w_spec] + list(extra_specs)
                 + [pl.BlockSpec(memory_space=pl.ANY)] * len(aliased),
        out_specs=out_specs,
        out_shape=out_shape,
        input_output_aliases={2 + len(extras): 0} if aliased else {},
        scratch_shapes=[pltpu.VMEM((tm, tn), F32)] if nk > 1 else [],
        compiler_params=_cparams(3, vmem_mib),
        name=name,
    )(a, w, *extras, *aliased)


def _store_heads(val, o_ref, g):
    for h in range(val.shape[1] // LANES):
        piece = val[:, h * LANES:(h + 1) * LANES].astype(o_ref.dtype)
        if g.per_row_mod:
            o_ref[:, h] = piece.reshape(g.B, g.T, LANES)
        else:
            o_ref[h] = piece


def _heads_out(g, n_heads, nh_tile, dtype, stack=None):
    shape = jax.ShapeDtypeStruct((g.B, n_heads, g.T, LANES), dtype)
    if g.per_row_mod:
        assert stack is None
        return shape, pl.BlockSpec((g.B, nh_tile, g.T, LANES), lambda i, j, k: (0, j, 0, 0))
    nt = g.T // g.tm
    if stack is None:
        return shape, pl.BlockSpec((None, nh_tile, g.tm, LANES), lambda i, j, k: (i // nt, j, i % nt, 0))
    n_layers, layer = stack
    shape = jax.ShapeDtypeStruct((n_layers,) + shape.shape, dtype)
    return shape, pl.BlockSpec((None, None, nh_tile, g.tm, LANES), lambda i, j, k: (layer, i // nt, j, i % nt, 0))


def _epi_heads_plain(acc, extras, outs, *, g, scale):
    _store_heads(acc if scale == 1.0 else acc * scale, outs[0], g)


def _epi_heads_rms(acc, extras, outs, *, g, scale):
    gain = extras[0][...]
    pieces = []
    for h in range(acc.shape[1] // LANES):
        x = acc[:, h * LANES:(h + 1) * LANES]
        y = x * lax.rsqrt(jnp.mean(x * x, axis=-1, keepdims=True) + NORM_EPS) * gain
        pieces.append(y if scale == 1.0 else y * scale)
    _store_heads(jnp.concatenate(pieces, axis=1), outs[0], g)


def _epi_rms_half(acc, extras, outs, *, scale):
    gain = extras[0][...]
    lo = lax.broadcasted_iota(jnp.int32, (1, LANES), 1) < (LANES // 2)
    pieces = []
    for h in range(acc.shape[1] // LANES):
        x = acc[:, h * LANES:(h + 1) * LANES]
        sq = x * x
        s_lo = jnp.sum(jnp.where(lo, sq, 0.0), axis=-1, keepdims=True)
        s_hi = jnp.sum(jnp.where(lo, 0.0, sq), axis=-1, keepdims=True)
        ms = jnp.where(lo, s_lo, s_hi) * (2.0 / LANES)
        y = x * lax.rsqrt(ms + NORM_EPS) * gain
        pieces.append(y if scale == 1.0 else y * scale)
    outs[0][...] = jnp.concatenate(pieces, axis=1).astype(outs[0].dtype)


def _epi_plain2d(acc, extras, outs):
    outs[0][...] = acc.astype(outs[0].dtype)


def _epi_sigmoid(acc, extras, outs):
    outs[0][...] = jax.nn.sigmoid(acc).astype(outs[0].dtype)


def _epi_relu2(acc, extras, outs):
    r = jnp.maximum(acc, 0.0)
    outs[0][...] = (r * r).astype(outs[0].dtype)


def _epi_residual(acc, extras, outs):
    x_ref, gate_ref = extras
    outs[0][...] = x_ref[...] + gate_ref[...] * acc


def _lf_kernel(w_ref, h_ref, b_ref, o_ref):
    u = lax.dot_general(w_ref[...].astype(BF16), h_ref[...], NT_DIMS, preferred_element_type=F32)
    o_ref[...] = _log_sigmoid(u + b_ref[...])


def _lf_proj(w_fa_t, h, b_fa):
    m, d_model = h.shape
    tm = _pick(m, 512)
    return pl.pallas_call(
        _lf_kernel,
        grid=(m // tm,),
        in_specs=[pl.BlockSpec((HEAD_PAD, d_model), lambda i: (0, 0)),
                  pl.BlockSpec((tm, d_model), lambda i: (i, 0)),
                  pl.BlockSpec((HEAD_PAD, 1), lambda i: (0, 0))],
        out_specs=pl.BlockSpec((HEAD_PAD, tm), lambda i: (0, i)),
        out_shape=jax.ShapeDtypeStruct((HEAD_PAD, m), F32),
        compiler_params=_cparams(1, 32),
        name="proj_logf",
    )(w_fa_t, h, b_fa)


def _ecum_kernel(x_ref, o_ref, loc_sc, tot_sc, *, n_chunks):
    x = x_ref[...].reshape(n_chunks * HEAD_PAD, LANES)
    u = _suffix_ones(LANES)
    hi, mid, lo = _split3(x)
    loc = (jnp.dot(hi, u, preferred_element_type=F32) + jnp.dot(mid, u, preferred_element_type=F32)
           + jnp.dot(lo, u, preferred_element_type=F32))
    loc_sc[...] = loc.reshape(n_chunks, HEAD_PAD, LANES)
    tot_sc[...] = jnp.sum(x, axis=-1, keepdims=True).reshape(n_chunks, HEAD_PAD, 1)

    def body(i, later):
        j = n_chunks - 1 - i
        o_ref[j] = loc_sc[j] + later
        return later + tot_sc[j]

    lax.fori_loop(0, n_chunks, body, jnp.zeros((HEAD_PAD, 1), F32))


def _ecum(x):
    nb, n_chunks = x.shape[0], x.shape[1]
    return pl.pallas_call(
        functools.partial(_ecum_kernel, n_chunks=n_chunks),
        grid=(nb,),
        in_specs=[pl.BlockSpec((None, n_chunks, HEAD_PAD, LANES), lambda b: (b, 0, 0, 0))],
        out_specs=pl.BlockSpec((None, n_chunks, HEAD_PAD, LANES), lambda b: (b, 0, 0, 0)),
        out_shape=jax.ShapeDtypeStruct(x.shape, F32),
        scratch_shapes=[pltpu.VMEM((n_chunks, HEAD_PAD, LANES), F32), pltpu.VMEM((n_chunks, HEAD_PAD, 1), F32)],
        compiler_params=_cparams(1, 32),
        name="logf_suffix_sum",
    )(x)


def _lf_gather_kernel(pt_ref, tab_ref, new_ref, o_ref, *, n_pages, n_heads):
    b = pl.program_id(0)
    sub = lax.broadcasted_iota(jnp.int32, (SUBLANES, LANES), 0)

    def body(j, carry):
        page = pt_ref[b, j]
        blk = lax.shift_right_logical(page, 3)
        row = jnp.bitwise_and(page, SUBLANES - 1)
        for h in range(n_heads):
            tile = tab_ref[h, blk]
            o_ref[j, h:h + 1, :] = jnp.sum(jnp.where(sub == row, tile, 0.0), axis=0, keepdims=True)
        o_ref[j, n_heads:HEAD_PAD, :] = jnp.zeros((HEAD_PAD - n_heads, LANES), F32)
        return carry

    lax.fori_loop(0, n_pages, body, 0)
    o_ref[n_pages] = new_ref[...]


def _lf_gather(page_table, table, lf_new):
    nb, n_pages = page_table.shape
    n_heads = table.shape[0]
    grid_spec = pltpu.PrefetchScalarGridSpec(
        num_scalar_prefetch=1,
        grid=(nb,),
        in_specs=[pl.BlockSpec(table.shape, lambda b, pt: (0, 0, 0, 0)),
                  pl.BlockSpec((None, HEAD_PAD, LANES), lambda b, pt: (b, 0, 0))],
        out_specs=pl.BlockSpec((None, n_pages + 1, HEAD_PAD, LANES), lambda b, pt: (b, 0, 0, 0)))
    return pl.pallas_call(
        functools.partial(_lf_gather_kernel, n_pages=n_pages, n_heads=n_heads),
        grid_spec=grid_spec,
        out_shape=jax.ShapeDtypeStruct((nb, n_pages + 1, HEAD_PAD, LANES), F32),
        compiler_params=_cparams(1, 40),
        name="logf_page_gather",
    )(page_table, table, lf_new)


def _softmax_step(s, v, carry):
    m, l, acc = carry
    m_new = jnp.maximum(m, jnp.max(s, axis=-1, keepdims=True))
    alpha = jnp.exp(m - m_new)
    p = jnp.exp(s - m_new)
    l = alpha * l + jnp.sum(p, axis=-1, keepdims=True)
    acc = alpha * acc + jnp.dot(p.astype(BF16), v, preferred_element_type=F32)
    return m_new, l, acc


def _causal_blocks(qi, tq, tk):
    return (qi * tq) // tk, -(-((qi + 1) * tq) // tk)


def _fox_prompt_kernel(q_ref, k_ref, v_ref, e_ref, o_ref, *, seq, tq, tk):
    for qi in range(seq // tq):
        q = q_ref[pl.ds(qi * tq, tq), :]

        def step(ki, carry, masked, q=q, qi=qi):
            k0 = ki * tk
            k = k_ref[pl.ds(k0, tk), :].astype(BF16)
            v = v_ref[pl.ds(k0, tk), :].astype(BF16)
            s = lax.dot_general(q, k, NT_DIMS, preferred_element_type=F32) + e_ref[:, pl.ds(k0, tk)]
            if masked:
                row = lax.broadcasted_iota(jnp.int32, (tq, tk), 0) + qi * tq
                col = lax.broadcasted_iota(jnp.int32, (tq, tk), 1) + k0
                s = jnp.where(col <= row, s, NEG_INF)
            return _softmax_step(s, v, carry)

        n_full, n_end = _causal_blocks(qi, tq, tk)
        carry = (jnp.full((tq, 1), NEG_INF, F32), jnp.zeros((tq, 1), F32), jnp.zeros((tq, LANES), F32))
        for kd in range(n_full, n_end):
            carry = step(kd, carry, True)
        for ki in range(n_full):
            carry = step(ki, carry, False)
        _, l, acc = carry
        o_ref[pl.ds(qi * tq, tq), :] = (acc / l).astype(o_ref.dtype)


def _kv_spec(kv, layer, seq):
    if kv.ndim == 4:
        return pl.BlockSpec((None, None, seq, LANES), lambda b, h: (b, h, 0, 0))
    return pl.BlockSpec((None, None, None, seq, LANES), lambda b, h: (layer, b, h, 0, 0))


def _fox_prompt(q, k, v, layer, e_rows, tq, tk):
    nb, nh, seq, _ = q.shape
    hm = lambda b, h: (b, h, 0, 0)
    return pl.pallas_call(
        functools.partial(_fox_prompt_kernel, seq=seq, tq=tq, tk=tk),
        grid=(nb, nh),
        in_specs=[pl.BlockSpec((None, None, seq, LANES), hm),
                  _kv_spec(k, layer, seq),
                  _kv_spec(v, layer, seq),
                  pl.BlockSpec((None, None, 1, seq), hm)],
        out_specs=pl.BlockSpec((seq, LANES), lambda b, h: (b, h)),
        out_shape=jax.ShapeDtypeStruct((nb * seq, nh * LANES), BF16),
        compiler_params=_cparams(2, 32),
        name="fox_prompt_attn",
    )(q, k, v, e_rows)


def _diff_lambda(lq1, lk1, lq2, lk2, lam_init):
    return (jnp.exp(jnp.sum(lq1[...] * lk1[...], axis=-1, keepdims=True))
            - jnp.exp(jnp.sum(lq2[...] * lk2[...], axis=-1, keepdims=True)) + lam_init)


def _diff_finish(acc0, l0, acc1, l1, lam, gain, lam_init):
    o = acc0 / l0 - lam * (acc1 / l1)
    o = o * lax.rsqrt(jnp.mean(o * o, axis=-1, keepdims=True) + NORM_EPS) * gain
    return o * (1.0 - lam_init)


def _diff_prompt_kernel(q_ref, k_ref, v_ref, lq1, lk1, lq2, lk2, gs_ref, o_ref, *, seq, tq, tk, lam_init, n_heads):
    h = pl.program_id(1)
    slope = jnp.exp2(-(8.0 / n_heads) * (h + 1).astype(F32) * jnp.ones((1, tk), F32))
    lam = _diff_lambda(lq1, lk1, lq2, lk2, lam_init)
    lo = lax.broadcasted_iota(jnp.int32, (1, LANES), 1) < (LANES // 2)
    for qi in range(seq // tq):
        q = q_ref[pl.ds(qi * tq, tq), :]
        zero = jnp.zeros_like(q)
        q2 = jnp.concatenate([jnp.where(lo, q, zero), jnp.where(lo, zero, q)], axis=0)

        def step(ki, carry, masked, q2=q2, qi=qi):
            k0 = ki * tk
            k = k_ref[pl.ds(k0, tk), :].astype(BF16)
            v = v_ref[pl.ds(k0, tk), :].astype(BF16)
            kpos = (lax.broadcasted_iota(jnp.int32, (1, tk), 1) + (k0 - qi * tq)).astype(F32)
            s = lax.dot_general(q2, k, NT_DIMS, preferred_element_type=F32) + slope * kpos
            if masked:
                row = lax.broadcasted_iota(jnp.int32, (2 * tq, tk), 0)
                row = jnp.where(row >= tq, row - tq, row) + qi * tq
                col = lax.broadcasted_iota(jnp.int32, (2 * tq, tk), 1) + k0
                s = jnp.where(col <= row, s, NEG_INF)
            return _softmax_step(s, v, carry)

        n_full, n_end = _causal_blocks(qi, tq, tk)
        carry = (jnp.full((2 * tq, 1), NEG_INF, F32), jnp.zeros((2 * tq, 1), F32), jnp.zeros((2 * tq, LANES), F32))
        for kd in range(n_full, n_end):
            carry = step(kd, carry, True)
        for ki in range(n_full):
            carry = step(ki, carry, False)
        _, l, acc = carry
        out = _diff_finish(acc[:tq], l[:tq], acc[tq:], l[tq:], lam, gs_ref[...], lam_init)
        o_ref[pl.ds(qi * tq, tq), :] = out.astype(o_ref.dtype)


def _diff_prompt(q, k, v, lams, g_sub, nb, seq, tq, tk, lam_init):
    nh = q.shape[1] // LANES
    blk = pl.BlockSpec((seq, LANES), lambda b, h: (b, h))
    vec64 = pl.BlockSpec((1, LANES // 2), lambda b, h: (0, 0))
    return pl.pallas_call(
        functools.partial(_diff_prompt_kernel, seq=seq, tq=tq, tk=tk, lam_init=lam_init, n_heads=nh),
        grid=(nb, nh),
        in_specs=[blk, blk, blk, vec64, vec64, vec64, vec64, pl.BlockSpec((1, LANES), lambda b, h: (0, 0))],
        out_specs=blk,
        out_shape=jax.ShapeDtypeStruct(q.shape, BF16),
        compiler_params=_cparams(2, 32),
        name="diff_prompt_attn",
    )(q, k, v, *lams, g_sub)


def _sb_block(z, v, later, acc, u, valid):
    l1mb = -jnp.maximum(z, 0.0) - jnp.log(1.0 + jnp.exp(-jnp.abs(z)))
    if valid is not None:
        l1mb = jnp.where(valid, l1mb, 0.0)
    after = _suffix_sum(l1mb, u) + later
    a = jnp.exp(l1mb + z + after)
    if valid is not None:
        a = jnp.where(valid, a, 0.0)
    acc = acc + jnp.dot(a.astype(BF16), v, preferred_element_type=F32)
    return later + jnp.sum(l1mb, axis=-1, keepdims=True), acc


def _sb_prompt_kernel(q_ref, k_ref, v_ref, o_ref, *, seq, tq, tk):
    u = _suffix_ones(tk)
    for qi in range(seq // tq):
        q = q_ref[pl.ds(qi * tq, tq), :]

        def step(ki, carry, masked, q=q, qi=qi):
            k0 = ki * tk
            k = k_ref[pl.ds(k0, tk), :].astype(BF16)
            v = v_ref[pl.ds(k0, tk), :].astype(BF16)
            z = lax.dot_general(q, k, NT_DIMS, preferred_element_type=F32)
            valid = None
            if masked:
                row = lax.broadcasted_iota(jnp.int32, (tq, tk), 0) + qi * tq
                col = lax.broadcasted_iota(jnp.int32, (tq, tk), 1) + k0
                valid = col < row
            return _sb_block(z, v, carry[0], carry[1], u, valid)

        n_full, n_end = _causal_blocks(qi, tq, tk)
        carry = (jnp.zeros((tq, 1), F32), jnp.zeros((tq, LANES), F32))
        for kd in reversed(range(n_full, n_end)):
            carry = step(kd, carry, True)
        for ki in reversed(range(n_full)):
            carry = step(ki, carry, False)
        o_ref[pl.ds(qi * tq, tq), :] = carry[1].astype(o_ref.dtype)


def _sb_prompt(q, k, v, layer, tq, tk):
    nb, nh, seq, _ = q.shape
    hm = lambda b, h: (b, h, 0, 0)
    return pl.pallas_call(
        functools.partial(_sb_prompt_kernel, seq=seq, tq=tq, tk=tk),
        grid=(nb, nh),
        in_specs=[pl.BlockSpec((None, None, seq, LANES), hm), _kv_spec(k, layer, seq), _kv_spec(v, layer, seq)],
        out_specs=pl.BlockSpec((seq, LANES), lambda b, h: (b, h)),
        out_shape=jax.ShapeDtypeStruct((nb * seq, nh * LANES), BF16),
        compiler_params=_cparams(2, 32),
        name="sb_prompt_attn",
    )(q, k, v)


def _page_specs(cache_block, layer, n_per_step, page_of):
    def spec(p):
        return pl.BlockSpec((None, None) + cache_block,
                            lambda b, j, pt: (layer, pt[b, page_of(j, p)]) + (0,) * len(cache_block))
    return [spec(p) for p in range(n_per_step)]


def _new_token_mask(rows):
    t = jnp.bitwise_and(lax.broadcasted_iota(jnp.int32, (rows, LANES), 0), SUBLANES - 1)
    return lax.broadcasted_iota(jnp.int32, (rows, LANES), 1), t


def _fox_dec_kernel(pt_ref, q_ref, e_ref, en_ref, kn_ref, vn_ref, *rest, n_slots, n_heads, n_steps):
    k_refs, v_refs = rest[:n_slots], rest[n_slots:2 * n_slots]
    o_ref, m_sc, l_sc, acc_sc = rest[2 * n_slots:]
    j = pl.program_id(1)
    rows = n_heads * SUBLANES

    @pl.when(j == 0)
    def _():
        m_sc[...] = jnp.full((rows, 1), NEG_INF, F32)
        l_sc[...] = jnp.zeros((rows, 1), F32)
        acc_sc[...] = jnp.zeros((rows, LANES), F32)

    def scores(k_of, e_of):
        return jnp.concatenate(
            [lax.dot_general(q_ref[h], k_of(h).astype(BF16), NT_DIMS, preferred_element_type=F32) + e_of(h)
             for h in range(n_heads)], axis=0)

    def update(s, v_of):
        m_prev = m_sc[...]
        m_new = jnp.maximum(m_prev, jnp.max(s, axis=-1, keepdims=True))
        alpha = jnp.exp(m_prev - m_new)
        p = jnp.exp(s - m_new)
        l_sc[...] = alpha * l_sc[...] + jnp.sum(p, axis=-1, keepdims=True)
        pb = p.astype(BF16)
        pv = []
        for h in range(n_heads):
            tot = None
            for pg in range(s.shape[1] // LANES):
                part = jnp.dot(pb[h * SUBLANES:(h + 1) * SUBLANES, pg * LANES:(pg + 1) * LANES],
                               v_of(pg, h).astype(BF16), preferred_element_type=F32)
                tot = part if tot is None else tot + part
            pv.append(tot)
        acc_sc[...] = alpha * acc_sc[...] + jnp.concatenate(pv, axis=0)
        m_sc[...] = m_new

    s_past = jnp.concatenate(
        [scores(lambda h, p=p: k_refs[p][h], lambda h, p=p: e_ref[p, h:h + 1, :]) for p in range(n_slots)], axis=1)
    update(s_past, lambda pg, h: v_refs[pg][h])

    @pl.when(j == n_steps - 1)
    def _():
        col, t = _new_token_mask(rows)
        s_new = scores(lambda h: kn_ref[h], lambda h: en_ref[h:h + 1, :])
        update(jnp.where(col <= t, s_new, NEG_INF), lambda pg, h: vn_ref[h])
        o_ref[...] = (acc_sc[...] / l_sc[...]).reshape(n_heads, SUBLANES, LANES).astype(o_ref.dtype)


def _fox_decode(page_table, q, e_all, k_new, v_new, cache_k, cache_v, layer, n_slots):
    nb, n_pages = page_table.shape
    n_heads = q.shape[1]
    n_steps = n_pages // n_slots
    page_blk = (n_heads, LANES, LANES)
    page_of = lambda j, p: j * n_slots + p
    per_b = lambda b, j, pt: (b, 0, 0, 0)
    grid_spec = pltpu.PrefetchScalarGridSpec(
        num_scalar_prefetch=1,
        grid=(nb, n_steps),
        in_specs=[pl.BlockSpec((None, n_heads, SUBLANES, LANES), per_b),
                  pl.BlockSpec((None, n_slots, HEAD_PAD, LANES), lambda b, j, pt: (b, j, 0, 0)),
                  pl.BlockSpec((None, None, HEAD_PAD, LANES), lambda b, j, pt: (b, n_pages, 0, 0)),
                  pl.BlockSpec((None,) + page_blk, per_b),
                  pl.BlockSpec((None,) + page_blk, per_b)]
                 + _page_specs(page_blk, layer, n_slots, page_of) + _page_specs(page_blk, layer, n_slots, page_of),
        out_specs=pl.BlockSpec((None, n_heads, SUBLANES, LANES), per_b),
        scratch_shapes=[pltpu.VMEM((n_heads * SUBLANES, 1), F32), pltpu.VMEM((n_heads * SUBLANES, 1), F32),
                        pltpu.VMEM((n_heads * SUBLANES, LANES), F32)])
    return pl.pallas_call(
        functools.partial(_fox_dec_kernel, n_slots=n_slots, n_heads=n_heads, n_steps=n_steps),
        grid_spec=grid_spec,
        out_shape=jax.ShapeDtypeStruct((nb, n_heads, SUBLANES, LANES), BF16),
        compiler_params=_cparams(2, 48),
        name="fox_decode_attn",
    )(page_table, q, e_all, e_all, k_new, v_new, *([cache_k] * n_slots), *([cache_v] * n_slots))


def _diff_dec_kernel(pt_ref, q_ref, kn_ref, vn_ref, lq1, lk1, lq2, lk2, gs_ref, *rest,
                     n_slots, n_heads, n_steps, past, lam_init):
    k_refs, v_refs = rest[:n_slots], rest[n_slots:2 * n_slots]
    o_ref, m_sc, l_sc, acc_sc = rest[2 * n_slots:]
    j = pl.program_id(1)
    hr = 2 * SUBLANES
    rows = n_heads * hr
    lo = lax.broadcasted_iota(jnp.int32, (1, LANES), 1) < (LANES // 2)

    @pl.when(j == 0)
    def _():
        m_sc[...] = jnp.full((rows, 1), NEG_INF, F32)
        l_sc[...] = jnp.zeros((rows, 1), F32)
        acc_sc[...] = jnp.zeros((rows, LANES), F32)

    col, t = _new_token_mask(hr)

    def scores(kt_of, first_pos):
        dist = ((past + t) - (first_pos + col)).astype(F32)
        out = []
        for h in range(n_heads):
            q = q_ref[h]
            zero = jnp.zeros_like(q)
            q2 = jnp.concatenate([jnp.where(lo, q, zero), jnp.where(lo, zero, q)], axis=0)
            s = jnp.dot(q2, kt_of(h).astype(BF16), preferred_element_type=F32)
            out.append(s - (2.0 ** (-(8.0 / n_heads) * (h + 1))) * dist)
        return jnp.concatenate(out, axis=0)

    ncol = LANES * n_heads
    slot_of_col, _ = _div_mod(lax.broadcasted_iota(jnp.int32, (LANES, ncol), 1), n_heads)
    spread = jnp.where(slot_of_col == lax.broadcasted_iota(jnp.int32, (LANES, ncol), 0), 1.0, 0.0).astype(BF16)
    head_of_row, _ = _div_mod(lax.broadcasted_iota(jnp.int32, (rows, ncol), 0), hr)
    own_head = head_of_row == _div_mod(lax.broadcasted_iota(jnp.int32, (rows, ncol), 1), n_heads)[1]

    def update(s, v_of):
        m_prev = m_sc[...]
        m_new = jnp.maximum(m_prev, jnp.max(s, axis=-1, keepdims=True))
        alpha = jnp.exp(m_prev - m_new)
        p = jnp.exp(s - m_new)
        l_sc[...] = alpha * l_sc[...] + jnp.sum(p, axis=-1, keepdims=True)
        pb = p.astype(BF16)
        tot = None
        for pg in range(s.shape[1] // LANES):
            wide = jnp.dot(pb[:, pg * LANES:(pg + 1) * LANES], spread, preferred_element_type=F32)
            wide = jnp.where(own_head, wide, 0.0).astype(BF16)
            part = jnp.dot(wide, v_of(pg).astype(BF16), preferred_element_type=F32)
            tot = part if tot is None else tot + part
        acc_sc[...] = alpha * acc_sc[...] + tot
        m_sc[...] = m_new

    s_past = jnp.concatenate(
        [scores(lambda h, p=p: k_refs[p][h], (j * n_slots + p) * LANES) for p in range(n_slots)], axis=1)
    update(s_past, lambda pg: v_refs[pg][...].reshape(ncol, LANES))

    @pl.when(j == n_steps - 1)
    def _():
        s_new = scores(lambda h: kn_ref[h], past)
        valid = jnp.concatenate([col <= t] * n_heads, axis=0)
        update(jnp.where(valid, s_new, NEG_INF), lambda pg: vn_ref[...].reshape(ncol, LANES))
        lam = _diff_lambda(lq1, lk1, lq2, lk2, lam_init)
        acc, l = acc_sc[...], l_sc[...]
        for h in range(n_heads):
            r0 = h * hr
            out = _diff_finish(acc[r0:r0 + SUBLANES], l[r0:r0 + SUBLANES], acc[r0 + SUBLANES:r0 + hr],
                               l[r0 + SUBLANES:r0 + hr], lam, gs_ref[...], lam_init)
            o_ref[h] = out.astype(o_ref.dtype)


def _diff_decode(page_table, q, kt_new, v_new, lams, g_sub, cache_kt, cache_v, layer, n_slots, lam_init):
    nb, n_pages = page_table.shape
    n_heads = q.shape[1]
    n_steps = n_pages // n_slots
    kt_blk, v_blk = (n_heads, LANES, LANES), (LANES, n_heads, LANES)
    page_of = lambda j, p: j * n_slots + p
    per_b = lambda b, j, pt: (b, 0, 0, 0)
    vec64 = pl.BlockSpec((1, LANES // 2), lambda b, j, pt: (0, 0))
    rows = n_heads * 2 * SUBLANES
    grid_spec = pltpu.PrefetchScalarGridSpec(
        num_scalar_prefetch=1,
        grid=(nb, n_steps),
        in_specs=[pl.BlockSpec((None, n_heads, SUBLANES, LANES), per_b),
                  pl.BlockSpec((None,) + kt_blk, per_b),
                  pl.BlockSpec((None,) + v_blk, per_b),
                  vec64, vec64, vec64, vec64, pl.BlockSpec((1, LANES), lambda b, j, pt: (0, 0))]
                 + _page_specs(kt_blk, layer, n_slots, page_of) + _page_specs(v_blk, layer, n_slots, page_of),
        out_specs=pl.BlockSpec((None, n_heads, SUBLANES, LANES), per_b),
        scratch_shapes=[pltpu.VMEM((rows, 1), F32), pltpu.VMEM((rows, 1), F32), pltpu.VMEM((rows, LANES), F32)])
    return pl.pallas_call(
        functools.partial(_diff_dec_kernel, n_slots=n_slots, n_heads=n_heads, n_steps=n_steps,
                          past=n_pages * LANES, lam_init=lam_init),
        grid_spec=grid_spec,
        out_shape=jax.ShapeDtypeStruct((nb, n_heads, SUBLANES, LANES), BF16),
        compiler_params=_cparams(2, 48),
        name="diff_decode_attn",
    )(page_table, q, kt_new, v_new, *lams, g_sub, *([cache_kt] * n_slots), *([cache_v] * n_slots))


def _sb_dec_kernel(pt_ref, q_ref, kn_ref, vn_ref, *rest, n_slots, n_heads, n_steps):
    k_refs, v_refs = rest[:n_slots], rest[n_slots:2 * n_slots]
    o_ref, later_sc, acc_sc = rest[2 * n_slots:]
    j = pl.program_id(1)
    rows = n_heads * SUBLANES
    u = _suffix_ones(LANES)

    def pages(k_of, v_of, n, valid):
        z = jnp.concatenate(
            [lax.dot_general(q_ref[h], k_of(pg, h).astype(BF16), NT_DIMS, preferred_element_type=F32)
             for pg in range(n) for h in range(n_heads)], axis=0)
        l1mb = -jnp.maximum(z, 0.0) - jnp.log(1.0 + jnp.exp(-jnp.abs(z)))
        if valid is not None:
            l1mb = jnp.where(valid, l1mb, 0.0)
        page_sum = jnp.sum(l1mb, axis=-1, keepdims=True)
        later = later_sc[...]
        laters = []
        for pg in range(n):
            laters.append(later)
            later = later + page_sum[pg * rows:(pg + 1) * rows]
        after = _suffix_sum(l1mb, u) + jnp.concatenate(laters, axis=0)
        a = jnp.exp(l1mb + z + after)
        if valid is not None:
            a = jnp.where(valid, a, 0.0)
        ab = a.astype(BF16)
        pv = []
        for h in range(n_heads):
            tot = None
            for pg in range(n):
                r0 = pg * rows + h * SUBLANES
                part = jnp.dot(ab[r0:r0 + SUBLANES], v_of(pg, h).astype(BF16), preferred_element_type=F32)
                tot = part if tot is None else tot + part
            pv.append(tot)
        acc_sc[...] += jnp.concatenate(pv, axis=0)
        later_sc[...] = later

    @pl.when(j == 0)
    def _():
        later_sc[...] = jnp.zeros((rows, 1), F32)
        acc_sc[...] = jnp.zeros((rows, LANES), F32)
        col, t = _new_token_mask(rows)
        pages(lambda pg, h: kn_ref[h], lambda pg, h: vn_ref[h], 1, col < t)

    pages(lambda pg, h: k_refs[pg][h], lambda pg, h: v_refs[pg][h], n_slots, None)

    @pl.when(j == n_steps - 1)
    def _():
        o_ref[...] = acc_sc[...].reshape(n_heads, SUBLANES, LANES).astype(o_ref.dtype)


def _sb_decode(page_table, q, k_new, v_new, cache_k, cache_v, layer, n_slots):
    nb, n_pages = page_table.shape
    n_heads = q.shape[1]
    n_steps = n_pages // n_slots
    page_blk = (n_heads, LANES, LANES)
    page_of = lambda j, p: n_pages - 1 - (j * n_slots + p)
    per_b = lambda b, j, pt: (b, 0, 0, 0)
    grid_spec = pltpu.PrefetchScalarGridSpec(
        num_scalar_prefetch=1,
        grid=(nb, n_steps),
        in_specs=[pl.BlockSpec((None, n_heads, SUBLANES, LANES), per_b),
                  pl.BlockSpec((None,) + page_blk, per_b),
                  pl.BlockSpec((None,) + page_blk, per_b)]
                 + _page_specs(page_blk, layer, n_slots, page_of) + _page_specs(page_blk, layer, n_slots, page_of),
        out_specs=pl.BlockSpec((None, n_heads, SUBLANES, LANES), per_b),
        scratch_shapes=[pltpu.VMEM((n_heads * SUBLANES, 1), F32), pltpu.VMEM((n_heads * SUBLANES, LANES), F32)])
    return pl.pallas_call(
        functools.partial(_sb_dec_kernel, n_slots=n_slots, n_heads=n_heads, n_steps=n_steps),
        grid_spec=grid_spec,
        out_shape=jax.ShapeDtypeStruct((nb, n_heads, SUBLANES, LANES), BF16),
        compiler_params=_cparams(2, 48),
        name="sb_decode_attn",
    )(page_table, q, k_new, v_new, *([cache_k] * n_slots), *([cache_v] * n_slots))


def _merge_kernel(oa, ob, oc, wa, wb, wc, g0, g1, g2, o_ref):
    pa = jnp.dot(oa[...], wa[...].astype(BF16), preferred_element_type=F32)
    pb = jnp.dot(ob[...], wb[...].astype(BF16), preferred_element_type=F32)
    pc = jnp.dot(oc[...], wc[...].astype(BF16), preferred_element_type=F32)
    o_ref[...] = (g0[...] * pa + g1[...] * pb + g2[...] * pc).astype(o_ref.dtype)


def _merge(oa, ob, oc, wa, wb, wc, layer, gates, tm, tn):
    m = oa.shape[0]
    d_model = wa.shape[2]
    nj = d_model // tn
    a_spec = lambda a: pl.BlockSpec((tm, a.shape[1]), lambda i, j: (i, 0))
    w_spec = lambda w: pl.BlockSpec((None, w.shape[1], tn), lambda i, j: (layer, 0, j))
    g_spec = lambda c: pl.BlockSpec((tm, tn), lambda i, j: (i, c * nj + j))
    return pl.pallas_call(
        _merge_kernel,
        grid=(m // tm, nj),
        in_specs=[a_spec(oa), a_spec(ob), a_spec(oc), w_spec(wa), w_spec(wb), w_spec(wc),
                  g_spec(0), g_spec(1), g_spec(2)],
        out_specs=pl.BlockSpec((tm, tn), lambda i, j: (i, j)),
        out_shape=jax.ShapeDtypeStruct((m, d_model), BF16),
        compiler_params=_cparams(2, 48),
        name="branch_merge",
    )(oa, ob, oc, wa, wb, wc, gates, gates, gates)


def _project(h, g, w_t, layer, name, col0, ncols, tn, epilogue, extras=(), extra_specs=(), heads=None, dtype=F32,
             stack=None, into=None):
    m, d_model = h.shape
    if heads is not None:
        shape, spec = _heads_out(g, heads, tn // LANES, dtype, stack)
    else:
        shape = jax.ShapeDtypeStruct((m, ncols), dtype)
        spec = pl.BlockSpec((g.tm, tn), lambda i, j, k: (i, j))
    return _mm(name, h, w_t, layer, col0, ncols, g.tm, tn, d_model, epilogue, list(extras), list(extra_specs),
               shape, spec, w_rows_are_outputs=True, into=into)


def _vec_spec(n):
    return pl.BlockSpec((1, n), lambda i, j, k: (0, 0))


def _layer_front(x, mod, g, p, dims, stack=None, prev=None):
    fox_w, dqk_w, dv_w, sb_w, d_model = dims
    kv = lambda name: dict(stack=stack, into=None if prev is None else prev[name])
    fh, sh = fox_w // LANES, sb_w // LANES
    h = _prenorm(x, p["g_norm1"], mod, g, 1, 0)
    tn = 512
    sc128 = 1.0 / math.sqrt(LANES)
    rms = lambda scale: functools.partial(_epi_heads_rms, g=g, scale=scale)
    plain = lambda scale: functools.partial(_epi_heads_plain, g=g, scale=scale)
    half = lambda scale: functools.partial(_epi_rms_half, scale=scale)
    out = {}
    proj = functools.partial(_project, h, g, p["w_in_t"], p["layer"])
    c = 0
    out["qa"] = proj("proj_fox_q", c, fox_w, tn, rms(sc128), [p["g_q_fox"]], [_vec_spec(LANES)], heads=fh, dtype=BF16); c += fox_w
    out["ka"] = proj("proj_fox_k", c, fox_w, tn, rms(1.0), [p["g_k_fox"]], [_vec_spec(LANES)], heads=fh, **kv("ka")); c += fox_w
    out["va"] = proj("proj_fox_v", c, fox_w, tn, plain(1.0), heads=fh, **kv("va")); c += fox_w
    out["qb"] = proj("proj_diff_q", c, dqk_w, tn, half(0.125), [p["g_q_diff"]], [_vec_spec(LANES)], dtype=BF16); c += dqk_w
    out["kb"] = proj("proj_diff_k", c, dqk_w, tn, half(1.0), [p["g_k_diff"]], [_vec_spec(LANES)]); c += dqk_w
    out["vb"] = proj("proj_diff_v", c, dv_w, tn, _epi_plain2d); c += dv_w
    out["qc"] = proj("proj_sb_q", c, sb_w, tn, plain(sc128), heads=sh, dtype=BF16); c += sb_w
    out["kc"] = proj("proj_sb_k", c, sb_w, tn, plain(1.0), heads=sh, **kv("kc")); c += sb_w
    out["vc"] = proj("proj_sb_v", c, sb_w, tn, plain(1.0), heads=sh, **kv("vc")); c += sb_w
    out["gates"] = proj("proj_gates", c, 3 * d_model, _pick(3 * d_model, 1024), _epi_sigmoid, dtype=BF16)
    out["lf_t"] = _lf_proj(p["w_fa_t"], h, p["b_fa"])
    return out


def _layer_back(x, mod, g, p, oa, ob, oc, gates):
    m, d_model = x.shape
    layer = p["layer"]
    merged = _merge(oa, ob, oc, p["w_br_fox"], p["w_br_diff"], p["w_br_sb"], layer, gates, g.tm, 256)
    tn = 512
    xspec = pl.BlockSpec((g.tm, tn), lambda i, j, k: (i, j))
    oshape = jax.ShapeDtypeStruct((m, d_model), F32)
    x1 = _mm("out_proj", merged, p["w_out"], layer, 0, d_model, g.tm, tn, d_model, _epi_residual,
             [x, mod], [xspec, _mod_spec(g, 2, d_model, tn)], oshape, xspec)
    h2 = _prenorm(x1, p["g_norm2"], mod, g, 4, 3)
    d_ff = p["w_up"].shape[2]
    f = _mm("mlp_up", h2, p["w_up"], layer, 0, d_ff, g.tm, tn, d_model, _epi_relu2, [], [],
            jax.ShapeDtypeStruct((m, d_ff), BF16), xspec)
    tn2 = _pick(d_model, 1024)
    xspec2 = pl.BlockSpec((g.tm, tn2), lambda i, j, k: (i, j))
    return _mm("mlp_down", f, p["w_down"], layer, 0, d_model, g.tm, tn2, _pick(d_ff, 2048), _epi_residual,
               [x1, mod], [xspec2, _mod_spec(g, 5, d_model, tn2)], oshape, xspec2)


def kernel(x_prompt, x_sample, cache_fox_k, cache_fox_v, cache_fox_logf, cache_diff_k, cache_diff_v, cache_sb_k, cache_sb_v, page_table, c_prompt, c_sample, w_ada, b_ada, g_norm1, w_in, b_fgate, g_q_fox, g_k_fox, g_q_diff, g_k_diff, lambda_q1, lambda_k1, lambda_q2, lambda_k2, g_sub_diff, w_br_fox, w_br_diff, w_br_sb, w_out, g_norm2, w_up, w_down):
    n_layers = w_ada.shape[0]
    nbp, seq, d_model = x_prompt.shape
    nbs, dseq, _ = x_sample.shape
    assert dseq == SUBLANES, "decode kernels hold the new tokens of a sequence in one sublane group"
    n_pages = page_table.shape[1]
    n_pool, page_size, fh, fd = cache_fox_k.shape[1:]
    dh, dqk = cache_diff_k.shape[3], cache_diff_k.shape[5]
    sh = cache_sb_k.shape[3]
    assert page_size == LANES and fd == LANES and 2 * dqk == LANES and cache_sb_k.shape[4] == LANES
    fox_w, dqk_w, dv_w, sb_w = fh * LANES, dh * LANES, dh * LANES, sh * LANES
    dims = (fox_w, dqk_w, dv_w, sb_w, d_model)
    past = n_pages * page_size
    mp, ms = nbp * seq, nbs * dseq

    gp = Group(nbp, seq, _pick(seq, 1024), False)
    gs = Group(nbs, dseq, ms, True)
    tq_fox, tq_diff, tq_sb = _pick(seq, 1024), _pick(seq, 512), _pick(seq, 512)
    tk_fox, tk_diff, tk_sb = _pick(seq, 512), _pick(seq, 512), _pick(seq, 256)
    n_slots = max(s for s in (8, 4, 2, 1) if n_pages % s == 0)

    n_c = nbp + nbs
    c_rows = -(-n_c // SUBLANES) * SUBLANES
    c_all = jnp.concatenate([c_prompt, c_sample, jnp.zeros((c_rows - n_c, d_model), F32)], axis=0)
    mod_all = _ada(c_all, w_ada, b_ada)

    fox_k_hm = jnp.transpose(cache_fox_k, (0, 1, 3, 2, 4))
    fox_v_hm = jnp.transpose(cache_fox_v, (0, 1, 3, 2, 4))
    sb_k_hm = jnp.transpose(cache_sb_k, (0, 1, 3, 2, 4))
    sb_v_hm = jnp.transpose(cache_sb_v, (0, 1, 3, 2, 4))
    diff_kt = jnp.transpose(cache_diff_k, (0, 1, 3, 4, 5, 2)).reshape(n_layers, n_pool, dh, LANES, page_size)
    logf_t = jnp.transpose(cache_fox_logf, (0, 3, 1, 2))
    pool_pad = -(-n_pool // SUBLANES) * SUBLANES

    c_fa = 3 * fox_w
    n_fa = fh
    yp, ys = x_prompt.reshape(mp, d_model), x_sample.reshape(ms, d_model)
    outs_p, outs_s = [], []
    w_fa_rows = jnp.transpose(w_in, (2, 0, 1))[c_fa:c_fa + n_fa]
    w_in_l = _w_in_by_layer(w_in, c_fa, n_fa, 256)
    w_down_bf = w_down.astype(BF16)
    for l in range(n_layers):
        lam_init = 0.8 - 0.6 * math.exp(-0.3 * l)
        pick_l = (jnp.arange(n_layers) == l)[None, :, None]
        p = {
            "layer": l,
            "w_in_t": w_in_l,
            "w_fa_t": jnp.pad(jnp.sum(jnp.where(pick_l, w_fa_rows, 0.0), axis=1), ((0, HEAD_PAD - n_fa), (0, 0))),
            "b_fa": jnp.pad(b_fgate[l], (0, HEAD_PAD - n_fa)).reshape(HEAD_PAD, 1),
            "g_norm1": g_norm1[l], "g_norm2": g_norm2[l],
            "g_q_fox": g_q_fox[l].reshape(1, LANES), "g_k_fox": g_k_fox[l].reshape(1, LANES),
            "g_q_diff": jnp.tile(g_q_diff[l], 2).reshape(1, LANES), "g_k_diff": jnp.tile(g_k_diff[l], 2).reshape(1, LANES),
            "w_br_fox": w_br_fox, "w_br_diff": w_br_diff, "w_br_sb": w_br_sb,
            "w_out": w_out, "w_up": w_up, "w_down": w_down_bf,
        }
        lams = [v[l].reshape(1, dqk) for v in (lambda_q1, lambda_k1, lambda_q2, lambda_k2)]
        g_sub = g_sub_diff[l].reshape(1, LANES)

        mod_p = mod_all[l, :nbp].reshape(nbp, 1, -1)
        fr = _layer_front(yp, mod_p, gp, p, dims, stack=(n_layers, l), prev=fr if l else None)
        lf_p = fr["lf_t"]
        x_e = lf_p.reshape(HEAD_PAD, nbp, seq // LANES, LANES).transpose(1, 2, 0, 3)
        e_p = _ecum(x_e).transpose(0, 2, 1, 3).reshape(nbp, HEAD_PAD, 1, seq)
        oa = _fox_prompt(fr["qa"], fr["ka"], fr["va"], l, e_p, tq_fox, tk_fox)
        ob = _diff_prompt(fr["qb"], fr["kb"], fr["vb"], lams, g_sub, nbp, seq, tq_diff, tk_diff, lam_init)
        oc = _sb_prompt(fr["qc"], fr["kc"], fr["vc"], l, tq_sb, tk_sb)
        yp = _layer_back(yp, mod_p, gp, p, oa, ob, oc, fr["gates"])
        hm_out = lambda a: jnp.transpose(a, (0, 2, 1, 3))
        outs_p.append((
            lf_p[:fh].reshape(fh, nbp, seq).transpose(1, 2, 0),
            fr["kb"].reshape(nbp, seq, dh, 2, dqk), fr["vb"].reshape(nbp, seq, dh, LANES)))

        mod_s = jnp.repeat(mod_all[l, nbp:nbp + nbs], dseq, axis=0)
        fs = _layer_front(ys, mod_s, gs, p, dims)
        lf_s = fs["lf_t"]
        lf_new = jnp.pad(lf_s.reshape(HEAD_PAD, nbs, dseq).transpose(1, 0, 2), ((0, 0), (0, 0), (0, LANES - dseq)))
        table = jnp.pad(logf_t[l], ((0, 0), (0, pool_pad - n_pool), (0, 0))).reshape(fh, pool_pad // SUBLANES, SUBLANES, LANES)
        e_s = _ecum(_lf_gather(page_table, table, lf_new))
        pad_keys = lambda a: jnp.pad(a, ((0, 0), (0, 0), (0, LANES - dseq), (0, 0)))
        oa = _fox_decode(page_table, fs["qa"], e_s, pad_keys(fs["ka"]), pad_keys(fs["va"]), fox_k_hm, fox_v_hm, l, n_slots)
        qb_hm = fs["qb"].reshape(nbs, dseq, dh, LANES).transpose(0, 2, 1, 3)
        kb_t = fs["kb"].reshape(nbs, dseq, dh, LANES).transpose(0, 2, 3, 1)
        kt_new = jnp.pad(kb_t, ((0, 0), (0, 0), (0, 0), (0, LANES - dseq)))
        vb_new = jnp.pad(fs["vb"].reshape(nbs, dseq, dh, LANES), ((0, 0), (0, LANES - dseq), (0, 0), (0, 0)))
        ob = _diff_decode(page_table, qb_hm, kt_new, vb_new, lams, g_sub, diff_kt, cache_diff_v, l, n_slots, lam_init)
        oc = _sb_decode(page_table, fs["qc"], pad_keys(fs["kc"]), pad_keys(fs["vc"]), sb_k_hm, sb_v_hm, l, n_slots)
        rows2d = lambda a: jnp.transpose(a, (0, 2, 1, 3)).reshape(ms, -1)
        ys = _layer_back(ys, mod_s, gs, p, rows2d(oa), rows2d(ob), rows2d(oc), fs["gates"])
        outs_s.append((
            hm_out(fs["ka"]), hm_out(fs["va"]), lf_s[:fh].reshape(fh, nbs, dseq).transpose(1, 2, 0),
            fs["kb"].reshape(nbs, dseq, dh, 2, dqk), fs["vb"].reshape(nbs, dseq, dh, LANES),
            hm_out(fs["kc"]), hm_out(fs["vc"])))

    stack = lambda outs: tuple(jnp.stack([o[i] for o in outs]) for i in range(len(outs[0])))
    p_lf, p_dk, p_dv = stack(outs_p)
    hm_stacked = lambda a: jnp.transpose(a, (0, 1, 3, 2, 4))
    prompt_rows = (hm_stacked(fr["ka"]), hm_stacked(fr["va"]), p_lf, p_dk, p_dv, hm_stacked(fr["kc"]), hm_stacked(fr["vc"]))
    return (yp.reshape(nbp, seq, d_model), ys.reshape(nbs, dseq, d_model)) + prompt_rows + stack(outs_s)
```

```python
import functools
import math
from typing import NamedTuple

import jax
import jax.numpy as jnp
from jax import lax
from jax.experimental import pallas as pl
from jax.experimental.pallas import tpu as pltpu

F32 = jnp.float32
BF16 = jnp.bfloat16
NORM_EPS = 1e-6
LANES = 128
SUBLANES = 8
HEAD_PAD = 16
NT_DIMS = (((1,), (1,)), ((), ()))
MIB = 1 << 20
NEG_INF = float("-inf")


class Group(NamedTuple):
    B: int
    T: int
    tm: int
    per_row_mod: bool


def _pick(dim, pref):
    t = min(pref, dim)
    while t >= LANES:
        if dim % t == 0 and t % LANES == 0:
            return t
        t -= LANES
    return dim


def _cparams(n_axes, vmem_mib=None):
    return pltpu.CompilerParams(
        dimension_semantics=("arbitrary",) * n_axes,
        vmem_limit_bytes=None if vmem_mib is None else vmem_mib * MIB)


def _log_sigmoid(x):
    return jnp.minimum(x, 0.0) - jnp.log(1.0 + jnp.exp(-jnp.abs(x)))


def _split3(x):
    hi = x.astype(BF16)
    r1 = x - hi.astype(F32)
    mid = r1.astype(BF16)
    lo = (r1 - mid.astype(F32)).astype(BF16)
    return hi, mid, lo


def _suffix_sum(x, u):
    hi = x.astype(BF16)
    lo = (x - hi.astype(F32)).astype(BF16)
    return jnp.dot(hi, u, preferred_element_type=F32) + jnp.dot(lo, u, preferred_element_type=F32)


def _div_mod(x, n):
    if n & (n - 1) == 0:
        return lax.shift_right_logical(x, n.bit_length() - 1), jnp.bitwise_and(x, n - 1)
    return lax.div(x, n), lax.rem(x, n)


def _suffix_ones(n):
    r = lax.broadcasted_iota(jnp.int32, (n, n), 0)
    c = lax.broadcasted_iota(jnp.int32, (n, n), 1)
    return (r > c).astype(BF16)


def _mod_spec(g, chunk, d_model, tn):
    nj = d_model // tn
    if g.per_row_mod:
        return pl.BlockSpec((g.tm, tn), lambda i, j, *_: (i, chunk * nj + j))
    nt = g.T // g.tm
    return pl.BlockSpec((None, 1, tn), lambda i, j, *_: (i // nt, 0, chunk * nj + j))


def _ada_kernel(c_ref, w_ref, b_ref, o_ref):
    c = c_ref[...]
    a = (c * jax.nn.sigmoid(c)).astype(BF16)
    o_ref[...] = jnp.dot(a, w_ref[...].astype(BF16), preferred_element_type=F32) + b_ref[...]


def _ada(c_all, w_ada, b_ada):
    n_layers, d_model, n = w_ada.shape
    rows = c_all.shape[0]
    tn = _pick(n, 512)
    return pl.pallas_call(
        _ada_kernel,
        grid=(n_layers, n // tn),
        in_specs=[pl.BlockSpec((rows, d_model), lambda l, j: (0, 0)),
                  pl.BlockSpec((None, d_model, tn), lambda l, j: (l, 0, j)),
                  pl.BlockSpec((None, 1, tn), lambda l, j: (l, 0, j))],
        out_specs=pl.BlockSpec((None, rows, tn), lambda l, j: (l, 0, j)),
        out_shape=jax.ShapeDtypeStruct((n_layers, rows, n), F32),
        compiler_params=_cparams(2, 40),
        name="ada_modulation",
    )(c_all, w_ada, b_ada.reshape(n_layers, 1, n))


def _sublane_transpose(tiles):
    tiles = list(tiles)
    axis = tiles[0].ndim - 2
    sub = lax.broadcasted_iota(jnp.int32, tiles[0].shape, axis)
    for d in (4, 2, 1):
        keep_low = jnp.bitwise_and(sub, d) == 0
        for a in range(SUBLANES):
            if a & d:
                continue
            lo, hi = tiles[a], tiles[a + d]
            tiles[a] = jnp.where(keep_low, lo, pltpu.roll(hi, d, axis))
            tiles[a + d] = jnp.where(keep_low, pltpu.roll(lo, SUBLANES - d, axis), hi)
    return tiles


def _w_in_split_kernel(w_ref, o_ref, *, tn, n_layers, n_chunks):
    per_col = n_chunks * n_layers
    groups = per_col // SUBLANES
    x = w_ref[...].reshape(tn // SUBLANES, SUBLANES, groups, SUBLANES, LANES)
    by_row = _sublane_transpose([x[:, i] for i in range(SUBLANES)])
    for c_lo in range(SUBLANES):
        for c_hi in range(groups):
            kc, l = divmod(c_hi * SUBLANES + c_lo, n_layers)
            rows = by_row[c_lo][:, c_hi].reshape(tn, LANES)
            o_ref[l, :, kc * LANES:(kc + 1) * LANES] = rows.astype(o_ref.dtype)


def _w_in_by_layer(w_in, skip0, n_skip, tn):
    n_layers, kdim, n = w_in.shape
    n_chunks = kdim // LANES
    per_col = n_chunks * n_layers
    rows = w_in.reshape(n_layers, n_chunks, LANES, n).transpose(3, 1, 0, 2).reshape(n * per_col, LANES)
    n_out = n - n_skip
    assert skip0 % tn == 0 and n_out % tn == 0

    def src(i):
        col = i * tn
        return (jnp.where(col >= skip0, col + n_skip, col) * per_col, 0)

    return pl.pallas_call(
        functools.partial(_w_in_split_kernel, tn=tn, n_layers=n_layers, n_chunks=n_chunks),
        grid=(n_out // tn,),
        in_specs=[pl.BlockSpec((pl.Element(tn * per_col), pl.Element(LANES)), src)],
        out_specs=pl.BlockSpec((n_layers, tn, kdim), lambda i: (0, i, 0)),
        out_shape=jax.ShapeDtypeStruct((n_layers, n_out, kdim), BF16),
        compiler_params=_cparams(1, 40),
        name="w_in_by_layer",
    )(rows)


def _prenorm_kernel(x_ref, g_ref, sc_ref, sh_ref, o_ref):
    x = x_ref[...]
    ms = jnp.mean(x * x, axis=-1, keepdims=True)
    y = x * lax.rsqrt(ms + NORM_EPS) * g_ref[...]
    o_ref[...] = (y * (1.0 + sc_ref[...]) + sh_ref[...]).astype(o_ref.dtype)


def _prenorm(x, gain, mod, g, scale_chunk, shift_chunk):
    m, d_model = x.shape
    tm = min(g.tm, 256)
    gg = g._replace(tm=tm)
    return pl.pallas_call(
        _prenorm_kernel,
        grid=(m // tm, 1),
        in_specs=[pl.BlockSpec((tm, d_model), lambda i, j: (i, 0)),
                  pl.BlockSpec((1, d_model), lambda i, j: (0, 0)),
                  _mod_spec(gg, scale_chunk, d_model, d_model),
                  _mod_spec(gg, shift_chunk, d_model, d_model)],
        out_specs=pl.BlockSpec((tm, d_model), lambda i, j: (i, 0)),
        out_shape=jax.ShapeDtypeStruct((m, d_model), BF16),
        compiler_params=_cparams(2, 40),
        name="mod_rmsnorm",
    )(x, gain.reshape(1, d_model), mod, mod)


def _mm_kernel(*refs, nk, n_extra, n_aliased, epilogue, w_rows_are_outputs):
    a_ref, w_ref = refs[0], refs[1]
    extras = refs[2:2 + n_extra]
    refs = refs[:2 + n_extra] + refs[2 + n_extra + n_aliased:]
    if w_rows_are_outputs:
        part = lax.dot_general(a_ref[...], w_ref[...].astype(BF16), NT_DIMS, preferred_element_type=F32)
    else:
        part = jnp.dot(a_ref[...], w_ref[...].astype(BF16), preferred_element_type=F32)
    if nk == 1:
        epilogue(part, extras, refs[2 + n_extra:])
        return
    outs, acc_ref = refs[2 + n_extra:-1], refs[-1]
    k = pl.program_id(2)

    @pl.when(k == 0)
    def _():
        acc_ref[...] = part

    @pl.when(k > 0)
    def _():
        acc_ref[...] += part

    @pl.when(k == nk - 1)
    def _():
        epilogue(acc_ref[...], extras, outs)


def _mm(name, a, w, layer, col0, ncols, tm, tn, tk, epilogue, extras, extra_specs, out_shape, out_specs,
        w_rows_are_outputs=False, into=None, vmem_mib=48):
    m, kdim = a.shape
    assert col0 % tn == 0 and ncols % tn == 0 and m % tm == 0 and kdim % tk == 0
    nk, c0 = kdim // tk, col0 // tn
    if w_rows_are_outputs:
        w_spec = pl.BlockSpec((None, tn, tk), lambda i, j, k: (layer, c0 + j, k))
    else:
        w_spec = pl.BlockSpec((None, tk, tn), lambda i, j, k: (layer, k, c0 + j))
    aliased = [] if into is None else [into]
    kern = functools.partial(_mm_kernel, nk=nk, n_extra=len(extras), n_aliased=len(aliased), epilogue=epilogue,
                             w_rows_are_outputs=w_rows_are_outputs)
    return pl.pallas_call(
        kern,
        grid=(m // tm, ncols // tn, nk),
        in_specs=[pl.BlockSpec((tm, tk), lambda i, j, k: (i, k)), w_spec] + list(extra_specs)
                 + [pl.BlockSpec(memory_space=pl.ANY)] * len(aliased),
        out_specs=out_specs,
        out_shape=out_shape,
        input_output_aliases={2 + len(extras): 0} if aliased else {},
        scratch_shapes=[pltpu.VMEM((tm, tn), F32)] if nk > 1 else [],
        compiler_params=_cparams(3, vmem_mib),
        name=name,
    )(a, w, *extras, *aliased)


def _store_heads(val, o_ref, g):
    for h in range(val.shape[1] // LANES):
        piece = val[:, h * LANES:(h + 1) * LANES].astype(o_ref.dtype)
        if g.per_row_mod:
            o_ref[:, h] = piece.reshape(g.B, g.T, LANES)
        else:
            o_ref[h] = piece


def _heads_out(g, n_heads, nh_tile, dtype, stack=None):
    shape = jax.ShapeDtypeStruct((g.B, n_heads, g.T, LANES), dtype)
    if g.per_row_mod:
        assert stack is None
        return shape, pl.BlockSpec((g.B, nh_tile, g.T, LANES), lambda i, j, k: (0, j, 0, 0))
    nt = g.T // g.tm
    if stack is None:
        return shape, pl.BlockSpec((None, nh_tile, g.tm, LANES), lambda i, j, k: (i // nt, j, i % nt, 0))
    n_layers, layer = stack
    shape = jax.ShapeDtypeStruct((n_layers,) + shape.shape, dtype)
    return shape, pl.BlockSpec((None, None, nh_tile, g.tm, LANES), lambda i, j, k: (layer, i // nt, j, i % nt, 0))


def _epi_heads_plain(acc, extras, outs, *, g, scale):
    _store_heads(acc if scale == 1.0 else acc * scale, outs[0], g)


def _epi_heads_rms(acc, extras, outs, *, g, scale):
    gain = extras[0][...]
    pieces = []
    for h in range(acc.shape[1] // LANES):
        x = acc[:, h * LANES:(h + 1) * LANES]
        y = x * lax.rsqrt(jnp.mean(x * x, axis=-1, keepdims=True) + NORM_EPS) * gain
        pieces.append(y if scale == 1.0 else y * scale)
    _store_heads(jnp.concatenate(pieces, axis=1), outs[0], g)


def _epi_rms_half(acc, extras, outs, *, scale):
    gain = extras[0][...]
    lo = lax.broadcasted_iota(jnp.int32, (1, LANES), 1) < (LANES // 2)
    pieces = []
    for h in range(acc.shape[1] // LANES):
        x = acc[:, h * LANES:(h + 1) * LANES]
        sq = x * x
        s_lo = jnp.sum(jnp.where(lo, sq, 0.0), axis=-1, keepdims=True)
        s_hi = jnp.sum(jnp.where(lo, 0.0, sq), axis=-1, keepdims=True)
        ms = jnp.where(lo, s_lo, s_hi) * (2.0 / LANES)
        y = x * lax.rsqrt(ms + NORM_EPS) * gain
        pieces.append(y if scale == 1.0 else y * scale)
    outs[0][...] = jnp.concatenate(pieces, axis=1).astype(outs[0].dtype)


def _epi_plain2d(acc, extras, outs):
    outs[0][...] = acc.astype(outs[0].dtype)


def _epi_sigmoid(acc, extras, outs):
    outs[0][...] = jax.nn.sigmoid(acc).astype(outs[0].dtype)


def _epi_relu2(acc, extras, outs):
    r = jnp.maximum(acc, 0.0)
    outs[0][...] = (r * r).astype(outs[0].dtype)


def _epi_residual(acc, extras, outs):
    x_ref, gate_ref = extras
    outs[0][...] = x_ref[...] + gate_ref[...] * acc


def _lf_kernel(w_ref, h_ref, b_ref, o_ref):
    u = lax.dot_general(w_ref[...].astype(BF16), h_ref[...], NT_DIMS, preferred_element_type=F32)
    o_ref[...] = _log_sigmoid(u + b_ref[...])


def _lf_proj(w_fa_t, h, b_fa):
    m, d_model = h.shape
    tm = _pick(m, 512)
    return pl.pallas_call(
        _lf_kernel,
        grid=(m // tm,),
        in_specs=[pl.BlockSpec((HEAD_PAD, d_model), lambda i: (0, 0)),
                  pl.BlockSpec((tm, d_model), lambda i: (i, 0)),
                  pl.BlockSpec((HEAD_PAD, 1), lambda i: (0, 0))],
        out_specs=pl.BlockSpec((HEAD_PAD, tm), lambda i: (0, i)),
        out_shape=jax.ShapeDtypeStruct((HEAD_PAD, m), F32),
        compiler_params=_cparams(1, 32),
        name="proj_logf",
    )(w_fa_t, h, b_fa)


def _ecum_kernel(x_ref, o_ref, loc_sc, tot_sc, *, n_chunks):
    x = x_ref[...].reshape(n_chunks * HEAD_PAD, LANES)
    u = _suffix_ones(LANES)
    hi, mid, lo = _split3(x)
    loc = (jnp.dot(hi, u, preferred_element_type=F32) + jnp.dot(mid, u, preferred_element_type=F32)
           + jnp.dot(lo, u, preferred_element_type=F32))
    loc_sc[...] = loc.reshape(n_chunks, HEAD_PAD, LANES)
    tot_sc[...] = jnp.sum(x, axis=-1, keepdims=True).reshape(n_chunks, HEAD_PAD, 1)

    def body(i, later):
        j = n_chunks - 1 - i
        o_ref[j] = loc_sc[j] + later
        return later + tot_sc[j]

    lax.fori_loop(0, n_chunks, body, jnp.zeros((HEAD_PAD, 1), F32))


def _ecum(x):
    nb, n_chunks = x.shape[0], x.shape[1]
    return pl.pallas_call(
        functools.partial(_ecum_kernel, n_chunks=n_chunks),
        grid=(nb,),
        in_specs=[pl.BlockSpec((None, n_chunks, HEAD_PAD, LANES), lambda b: (b, 0, 0, 0))],
        out_specs=pl.BlockSpec((None, n_chunks, HEAD_PAD, LANES), lambda b: (b, 0, 0, 0)),
        out_shape=jax.ShapeDtypeStruct(x.shape, F32),
        scratch_shapes=[pltpu.VMEM((n_chunks, HEAD_PAD, LANES), F32), pltpu.VMEM((n_chunks, HEAD_PAD, 1), F32)],
        compiler_params=_cparams(1, 32),
        name="logf_suffix_sum",
    )(x)


def _lf_gather_kernel(pt_ref, tab_ref, new_ref, o_ref, *, n_pages, n_heads):
    b = pl.program_id(0)
    sub = lax.broadcasted_iota(jnp.int32, (SUBLANES, LANES), 0)

    def body(j, carry):
        page = pt_ref[b, j]
        blk = lax.shift_right_logical(page, 3)
        row = jnp.bitwise_and(page, SUBLANES - 1)
        for h in range(n_heads):
            tile = tab_ref[h, blk]
            o_ref[j, h:h + 1, :] = jnp.sum(jnp.where(sub == row, tile, 0.0), axis=0, keepdims=True)
        o_ref[j, n_heads:HEAD_PAD, :] = jnp.zeros((HEAD_PAD - n_heads, LANES), F32)
        return carry

    lax.fori_loop(0, n_pages, body, 0)
    o_ref[n_pages] = new_ref[...]


def _lf_gather(page_table, table, lf_new):
    nb, n_pages = page_table.shape
    n_heads = table.shape[0]
    grid_spec = pltpu.PrefetchScalarGridSpec(
        num_scalar_prefetch=1,
        grid=(nb,),
        in_specs=[pl.BlockSpec(table.shape, lambda b, pt: (0, 0, 0, 0)),
                  pl.BlockSpec((None, HEAD_PAD, LANES), lambda b, pt: (b, 0, 0))],
        out_specs=pl.BlockSpec((None, n_pages + 1, HEAD_PAD, LANES), lambda b, pt: (b, 0, 0, 0)))
    return pl.pallas_call(
        functools.partial(_lf_gather_kernel, n_pages=n_pages, n_heads=n_heads),
        grid_spec=grid_spec,
        out_shape=jax.ShapeDtypeStruct((nb, n_pages + 1, HEAD_PAD, LANES), F32),
        compiler_params=_cparams(1, 40),
        name="logf_page_gather",
    )(page_table, table, lf_new)


def _softmax_step(s, v, carry):
    m, l, acc = carry
    m_new = jnp.maximum(m, jnp.max(s, axis=-1, keepdims=True))
    alpha = jnp.exp(m - m_new)
    p = jnp.exp(s - m_new)
    l = alpha * l + jnp.sum(p, axis=-1, keepdims=True)
    acc = alpha * acc + jnp.dot(p.astype(BF16), v, preferred_element_type=F32)
    return m_new, l, acc


def _causal_blocks(qi, tq, tk):
    return (qi * tq) // tk, -(-((qi + 1) * tq) // tk)


def _fox_prompt_kernel(q_ref, k_ref, v_ref, e_ref, o_ref, *, seq, tq, tk):
    for qi in range(seq // tq):
        q = q_ref[pl.ds(qi * tq, tq), :]

        def step(ki, carry, masked, q=q, qi=qi):
            k0 = ki * tk
            k = k_ref[pl.ds(k0, tk), :].astype(BF16)
            v = v_ref[pl.ds(k0, tk), :].astype(BF16)
            s = lax.dot_general(q, k, NT_DIMS, preferred_element_type=F32) + e_ref[:, pl.ds(k0, tk)]
            if masked:
                row = lax.broadcasted_iota(jnp.int32, (tq, tk), 0) + qi * tq
                col = lax.broadcasted_iota(jnp.int32, (tq, tk), 1) + k0
                s = jnp.where(col <= row, s, NEG_INF)
            return _softmax_step(s, v, carry)

        n_full, n_end = _causal_blocks(qi, tq, tk)
        carry = (jnp.full((tq, 1), NEG_INF, F32), jnp.zeros((tq, 1), F32), jnp.zeros((tq, LANES), F32))
        for kd in range(n_full, n_end):
            carry = step(kd, carry, True)
        for ki in range(n_full):
            carry = step(ki, carry, False)
        _, l, acc = carry
        o_ref[pl.ds(qi * tq, tq), :] = (acc / l).astype(o_ref.dtype)


def _kv_spec(kv, layer, seq):
    if kv.ndim == 4:
        return pl.BlockSpec((None, None, seq, LANES), lambda b, h: (b, h, 0, 0))
    return pl.BlockSpec((None, None, None, seq, LANES), lambda b, h: (layer, b, h, 0, 0))


def _fox_prompt(q, k, v, layer, e_rows, tq, tk):
    nb, nh, seq, _ = q.shape
    hm = lambda b, h: (b, h, 0, 0)
    return pl.pallas_call(
        functools.partial(_fox_prompt_kernel, seq=seq, tq=tq, tk=tk),
        grid=(nb, nh),
        in_specs=[pl.BlockSpec((None, None, seq, LANES), hm),
                  _kv_spec(k, layer, seq),
                  _kv_spec(v, layer, seq),
                  pl.BlockSpec((None, None, 1, seq), hm)],
        out_specs=pl.BlockSpec((seq, LANES), lambda b, h: (b, h)),
        out_shape=jax.ShapeDtypeStruct((nb * seq, nh * LANES), BF16),
        compiler_params=_cparams(2, 32),
        name="fox_prompt_attn",
    )(q, k, v, e_rows)


def _diff_lambda(lq1, lk1, lq2, lk2, lam_init):
    return (jnp.exp(jnp.sum(lq1[...] * lk1[...], axis=-1, keepdims=True))
            - jnp.exp(jnp.sum(lq2[...] * lk2[...], axis=-1, keepdims=True)) + lam_init)


def _diff_finish(acc0, l0, acc1, l1, lam, gain, lam_init):
    o = acc0 / l0 - lam * (acc1 / l1)
    o = o * lax.rsqrt(jnp.mean(o * o, axis=-1, keepdims=True) + NORM_EPS) * gain
    return o * (1.0 - lam_init)


def _diff_prompt_kernel(q_ref, k_ref, v_ref, lq1, lk1, lq2, lk2, gs_ref, o_ref, *, seq, tq, tk, lam_init, n_heads):
    h = pl.program_id(1)
    slope = jnp.exp2(-(8.0 / n_heads) * (h + 1).astype(F32) * jnp.ones((1, tk), F32))
    lam = _diff_lambda(lq1, lk1, lq2, lk2, lam_init)
    lo = lax.broadcasted_iota(jnp.int32, (1, LANES), 1) < (LANES // 2)
    for qi in range(seq // tq):
        q = q_ref[pl.ds(qi * tq, tq), :]
        zero = jnp.zeros_like(q)
        q2 = jnp.concatenate([jnp.where(lo, q, zero), jnp.where(lo, zero, q)], axis=0)

        def step(ki, carry, masked, q2=q2, qi=qi):
            k0 = ki * tk
            k = k_ref[pl.ds(k0, tk), :].astype(BF16)
            v = v_ref[pl.ds(k0, tk), :].astype(BF16)
            kpos = (lax.broadcasted_iota(jnp.int32, (1, tk), 1) + (k0 - qi * tq)).astype(F32)
            s = lax.dot_general(q2, k, NT_DIMS, preferred_element_type=F32) + slope * kpos
            if masked:
                row = lax.broadcasted_iota(jnp.int32, (2 * tq, tk), 0)
                row = jnp.where(row >= tq, row - tq, row) + qi * tq
                col = lax.broadcasted_iota(jnp.int32, (2 * tq, tk), 1) + k0
                s = jnp.where(col <= row, s, NEG_INF)
            return _softmax_step(s, v, carry)

        n_full, n_end = _causal_blocks(qi, tq, tk)
        carry = (jnp.full((2 * tq, 1), NEG_INF, F32), jnp.zeros((2 * tq, 1), F32), jnp.zeros((2 * tq, LANES), F32))
        for kd in range(n_full, n_end):
            carry = step(kd, carry, True)
        for ki in range(n_full):
            carry = step(ki, carry, False)
        _, l, acc = carry
        out = _diff_finish(acc[:tq], l[:tq], acc[tq:], l[tq:], lam, gs_ref[...], lam_init)
        o_ref[pl.ds(qi * tq, tq), :] = out.astype(o_ref.dtype)


def _diff_prompt(q, k, v, lams, g_sub, nb, seq, tq, tk, lam_init):
    nh = q.shape[1] // LANES
    blk = pl.BlockSpec((seq, LANES), lambda b, h: (b, h))
    vec64 = pl.BlockSpec((1, LANES // 2), lambda b, h: (0, 0))
    return pl.pallas_call(
        functools.partial(_diff_prompt_kernel, seq=seq, tq=tq, tk=tk, lam_init=lam_init, n_heads=nh),
        grid=(nb, nh),
        in_specs=[blk, blk, blk, vec64, vec64, vec64, vec64, pl.BlockSpec((1, LANES), lambda b, h: (0, 0))],
        out_specs=blk,
        out_shape=jax.ShapeDtypeStruct(q.shape, BF16),
        compiler_params=_cparams(2, 32),
        name="diff_prompt_attn",
    )(q, k, v, *lams, g_sub)


def _sb_block(z, v, later, acc, u, valid):
    l1mb = -jnp.maximum(z, 0.0) - jnp.log(1.0 + jnp.exp(-jnp.abs(z)))
    if valid is not None:
        l1mb = jnp.where(valid, l1mb, 0.0)
    after = _suffix_sum(l1mb, u) + later
    a = jnp.exp(l1mb + z + after)
    if valid is not None:
        a = jnp.where(valid, a, 0.0)
    acc = acc + jnp.dot(a.astype(BF16), v, preferred_element_type=F32)
    return later + jnp.sum(l1mb, axis=-1, keepdims=True), acc


def _sb_prompt_kernel(q_ref, k_ref, v_ref, o_ref, *, seq, tq, tk):
    u = _suffix_ones(tk)
    for qi in range(seq // tq):
        q = q_ref[pl.ds(qi * tq, tq), :]

        def step(ki, carry, masked, q=q, qi=qi):
            k0 = ki * tk
            k = k_ref[pl.ds(k0, tk), :].astype(BF16)
            v = v_ref[pl.ds(k0, tk), :].astype(BF16)
            z = lax.dot_general(q, k, NT_DIMS, preferred_element_type=F32)
            valid = None
            if masked:
                row = lax.broadcasted_iota(jnp.int32, (tq, tk), 0) + qi * tq
                col = lax.broadcasted_iota(jnp.int32, (tq, tk), 1) + k0
                valid = col < row
            return _sb_block(z, v, carry[0], carry[1], u, valid)

        n_full, n_end = _causal_blocks(qi, tq, tk)
        carry = (jnp.zeros((tq, 1), F32), jnp.zeros((tq, LANES), F32))
        for kd in reversed(range(n_full, n_end)):
            carry = step(kd, carry, True)
        for ki in reversed(range(n_full)):
            carry = step(ki, carry, False)
        o_ref[pl.ds(qi * tq, tq), :] = carry[1].astype(o_ref.dtype)


def _sb_prompt(q, k, v, layer, tq, tk):
    nb, nh, seq, _ = q.shape
    hm = lambda b, h: (b, h, 0, 0)
    return pl.pallas_call(
        functools.partial(_sb_prompt_kernel, seq=seq, tq=tq, tk=tk),
        grid=(nb, nh),
        in_specs=[pl.BlockSpec((None, None, seq, LANES), hm), _kv_spec(k, layer, seq), _kv_spec(v, layer, seq)],
        out_specs=pl.BlockSpec((seq, LANES), lambda b, h: (b, h)),
        out_shape=jax.ShapeDtypeStruct((nb * seq, nh * LANES), BF16),
        compiler_params=_cparams(2, 32),
        name="sb_prompt_attn",
    )(q, k, v)


def _page_specs(cache_block, layer, n_per_step, page_of):
    def spec(p):
        return pl.BlockSpec((None, None) + cache_block,
                            lambda b, j, pt: (layer, pt[b, page_of(j, p)]) + (0,) * len(cache_block))
    return [spec(p) for p in range(n_per_step)]


def _new_token_mask(rows):
    t = jnp.bitwise_and(lax.broadcasted_iota(jnp.int32, (rows, LANES), 0), SUBLANES - 1)
    return lax.broadcasted_iota(jnp.int32, (rows, LANES), 1), t


def _fox_dec_kernel(pt_ref, q_ref, e_ref, en_ref, kn_ref, vn_ref, *rest, n_slots, n_heads, n_steps):
    k_refs, v_refs = rest[:n_slots], rest[n_slots:2 * n_slots]
    o_ref, m_sc, l_sc, acc_sc = rest[2 * n_slots:]
    j = pl.program_id(1)
    rows = n_heads * SUBLANES

    @pl.when(j == 0)
    def _():
        m_sc[...] = jnp.full((rows, 1), NEG_INF, F32)
        l_sc[...] = jnp.zeros((rows, 1), F32)
        acc_sc[...] = jnp.zeros((rows, LANES), F32)

    def scores(k_of, e_of):
        return jnp.concatenate(
            [lax.dot_general(q_ref[h], k_of(h).astype(BF16), NT_DIMS, preferred_element_type=F32) + e_of(h)
             for h in range(n_heads)], axis=0)

    def update(s, v_of):
        m_prev = m_sc[...]
        m_new = jnp.maximum(m_prev, jnp.max(s, axis=-1, keepdims=True))
        alpha = jnp.exp(m_prev - m_new)
        p = jnp.exp(s - m_new)
        l_sc[...] = alpha * l_sc[...] + jnp.sum(p, axis=-1, keepdims=True)
        pb = p.astype(BF16)
        pv = []
        for h in range(n_heads):
            tot = None
            for pg in range(s.shape[1] // LANES):
                part = jnp.dot(pb[h * SUBLANES:(h + 1) * SUBLANES, pg * LANES:(pg + 1) * LANES],
                               v_of(pg, h).astype(BF16), preferred_element_type=F32)
                tot = part if tot is None else tot + part
            pv.append(tot)
        acc_sc[...] = alpha * acc_sc[...] + jnp.concatenate(pv, axis=0)
        m_sc[...] = m_new

    s_past = jnp.concatenate(
        [scores(lambda h, p=p: k_refs[p][h], lambda h, p=p: e_ref[p, h:h + 1, :]) for p in range(n_slots)], axis=1)
    update(s_past, lambda pg, h: v_refs[pg][h])

    @pl.when(j == n_steps - 1)
    def _():
        col, t = _new_token_mask(rows)
        s_new = scores(lambda h: kn_ref[h], lambda h: en_ref[h:h + 1, :])
        update(jnp.where(col <= t, s_new, NEG_INF), lambda pg, h: vn_ref[h])
        o_ref[...] = (acc_sc[...] / l_sc[...]).reshape(n_heads, SUBLANES, LANES).astype(o_ref.dtype)


def _fox_decode(page_table, q, e_all, k_new, v_new, cache_k, cache_v, layer, n_slots):
    nb, n_pages = page_table.shape
    n_heads = q.shape[1]
    n_steps = n_pages // n_slots
    page_blk = (n_heads, LANES, LANES)
    page_of = lambda j, p: j * n_slots + p
    per_b = lambda b, j, pt: (b, 0, 0, 0)
    grid_spec = pltpu.PrefetchScalarGridSpec(
        num_scalar_prefetch=1,
        grid=(nb, n_steps),
        in_specs=[pl.BlockSpec((None, n_heads, SUBLANES, LANES), per_b),
                  pl.BlockSpec((None, n_slots, HEAD_PAD, LANES), lambda b, j, pt: (b, j, 0, 0)),
                  pl.BlockSpec((None, None, HEAD_PAD, LANES), lambda b, j, pt: (b, n_pages, 0, 0)),
                  pl.BlockSpec((None,) + page_blk, per_b),
                  pl.BlockSpec((None,) + page_blk, per_b)]
                 + _page_specs(page_blk, layer, n_slots, page_of) + _page_specs(page_blk, layer, n_slots, page_of),
        out_specs=pl.BlockSpec((None, n_heads, SUBLANES, LANES), per_b),
        scratch_shapes=[pltpu.VMEM((n_heads * SUBLANES, 1), F32), pltpu.VMEM((n_heads * SUBLANES, 1), F32),
                        pltpu.VMEM((n_heads * SUBLANES, LANES), F32)])
    return pl.pallas_call(
        functools.partial(_fox_dec_kernel, n_slots=n_slots, n_heads=n_heads, n_steps=n_steps),
        grid_spec=grid_spec,
        out_shape=jax.ShapeDtypeStruct((nb, n_heads, SUBLANES, LANES), BF16),
        compiler_params=_cparams(2, 48),
        name="fox_decode_attn",
    )(page_table, q, e_all, e_all, k_new, v_new, *([cache_k] * n_slots), *([cache_v] * n_slots))


def _diff_dec_kernel(pt_ref, q_ref, kn_ref, vn_ref, lq1, lk1, lq2, lk2, gs_ref, *rest,
                     n_slots, n_heads, n_steps, past, lam_init):
    k_refs, v_refs = rest[:n_slots], rest[n_slots:2 * n_slots]
    o_ref, m_sc, l_sc, acc_sc = rest[2 * n_slots:]
    j = pl.program_id(1)
    hr = 2 * SUBLANES
    rows = n_heads * hr
    lo = lax.broadcasted_iota(jnp.int32, (1, LANES), 1) < (LANES // 2)

    @pl.when(j == 0)
    def _():
        m_sc[...] = jnp.full((rows, 1), NEG_INF, F32)
        l_sc[...] = jnp.zeros((rows, 1), F32)
        acc_sc[...] = jnp.zeros((rows, LANES), F32)

    col, t = _new_token_mask(hr)

    def scores(kt_of, first_pos):
        dist = ((past + t) - (first_pos + col)).astype(F32)
        out = []
        for h in range(n_heads):
            q = q_ref[h]
            zero = jnp.zeros_like(q)
            q2 = jnp.concatenate([jnp.where(lo, q, zero), jnp.where(lo, zero, q)], axis=0)
            s = jnp.dot(q2, kt_of(h).astype(BF16), preferred_element_type=F32)
            out.append(s - (2.0 ** (-(8.0 / n_heads) * (h + 1))) * dist)
        return jnp.concatenate(out, axis=0)

    ncol = LANES * n_heads
    slot_of_col, _ = _div_mod(lax.broadcasted_iota(jnp.int32, (LANES, ncol), 1), n_heads)
    spread = jnp.where(slot_of_col == lax.broadcasted_iota(jnp.int32, (LANES, ncol), 0), 1.0, 0.0).astype(BF16)
    head_of_row, _ = _div_mod(lax.broadcasted_iota(jnp.int32, (rows, ncol), 0), hr)
    own_head = head_of_row == _div_mod(lax.broadcasted_iota(jnp.int32, (rows, ncol), 1), n_heads)[1]

    def update(s, v_of):
        m_prev = m_sc[...]
        m_new = jnp.maximum(m_prev, jnp.max(s, axis=-1, keepdims=True))
        alpha = jnp.exp(m_prev - m_new)
        p = jnp.exp(s - m_new)
        l_sc[...] = alpha * l_sc[...] + jnp.sum(p, axis=-1, keepdims=True)
        pb = p.astype(BF16)
        tot = None
        for pg in range(s.shape[1] // LANES):
            wide = jnp.dot(pb[:, pg * LANES:(pg + 1) * LANES], spread, preferred_element_type=F32)
            wide = jnp.where(own_head, wide, 0.0).astype(BF16)
            part = jnp.dot(wide, v_of(pg).astype(BF16), preferred_element_type=F32)
            tot = part if tot is None else tot + part
        acc_sc[...] = alpha * acc_sc[...] + tot
        m_sc[...] = m_new

    s_past = jnp.concatenate(
        [scores(lambda h, p=p: k_refs[p][h], (j * n_slots + p) * LANES) for p in range(n_slots)], axis=1)
    update(s_past, lambda pg: v_refs[pg][...].reshape(ncol, LANES))

    @pl.when(j == n_steps - 1)
    def _():
        s_new = scores(lambda h: kn_ref[h], past)
        valid = jnp.concatenate([col <= t] * n_heads, axis=0)
        update(jnp.where(valid, s_new, NEG_INF), lambda pg: vn_ref[...].reshape(ncol, LANES))
        lam = _diff_lambda(lq1, lk1, lq2, lk2, lam_init)
        acc, l = acc_sc[...], l_sc[...]
        for h in range(n_heads):
            r0 = h * hr
            out = _diff_finish(acc[r0:r0 + SUBLANES], l[r0:r0 + SUBLANES], acc[r0 + SUBLANES:r0 + hr],
                               l[r0 + SUBLANES:r0 + hr], lam, gs_ref[...], lam_init)
            o_ref[h] = out.astype(o_ref.dtype)


def _diff_decode(page_table, q, kt_new, v_new, lams, g_sub, cache_kt, cache_v, layer, n_slots, lam_init):
    nb, n_pages = page_table.shape
    n_heads = q.shape[1]
    n_steps = n_pages // n_slots
    kt_blk, v_blk = (n_heads, LANES, LANES), (LANES, n_heads, LANES)
    page_of = lambda j, p: j * n_slots + p
    per_b = lambda b, j, pt: (b, 0, 0, 0)
    vec64 = pl.BlockSpec((1, LANES // 2), lambda b, j, pt: (0, 0))
    rows = n_heads * 2 * SUBLANES
    grid_spec = pltpu.PrefetchScalarGridSpec(
        num_scalar_prefetch=1,
        grid=(nb, n_steps),
        in_specs=[pl.BlockSpec((None, n_heads, SUBLANES, LANES), per_b),
                  pl.BlockSpec((None,) + kt_blk, per_b),
                  pl.BlockSpec((None,) + v_blk, per_b),
                  vec64, vec64, vec64, vec64, pl.BlockSpec((1, LANES), lambda b, j, pt: (0, 0))]
                 + _page_specs(kt_blk, layer, n_slots, page_of) + _page_specs(v_blk, layer, n_slots, page_of),
        out_specs=pl.BlockSpec((None, n_heads, SUBLANES, LANES), per_b),
        scratch_shapes=[pltpu.VMEM((rows, 1), F32), pltpu.VMEM((rows, 1), F32), pltpu.VMEM((rows, LANES), F32)])
    return pl.pallas_call(
        functools.partial(_diff_dec_kernel, n_slots=n_slots, n_heads=n_heads, n_steps=n_steps,
                          past=n_pages * LANES, lam_init=lam_init),
        grid_spec=grid_spec,
        out_shape=jax.ShapeDtypeStruct((nb, n_heads, SUBLANES, LANES), BF16),
        compiler_params=_cparams(2, 48),
        name="diff_decode_attn",
    )(page_table, q, kt_new, v_new, *lams, g_sub, *([cache_kt] * n_slots), *([cache_v] * n_slots))


def _sb_dec_kernel(pt_ref, q_ref, kn_ref, vn_ref, *rest, n_slots, n_heads, n_steps):
    k_refs, v_refs = rest[:n_slots], rest[n_slots:2 * n_slots]
    o_ref, later_sc, acc_sc = rest[2 * n_slots:]
    j = pl.program_id(1)
    rows = n_heads * SUBLANES
    u = _suffix_ones(LANES)

    def pages(k_of, v_of, n, valid):
        z = jnp.concatenate(
            [lax.dot_general(q_ref[h], k_of(pg, h).astype(BF16), NT_DIMS, preferred_element_type=F32)
             for pg in range(n) for h in range(n_heads)], axis=0)
        l1mb = -jnp.maximum(z, 0.0) - jnp.log(1.0 + jnp.exp(-jnp.abs(z)))
        if valid is not None:
            l1mb = jnp.where(valid, l1mb, 0.0)
        page_sum = jnp.sum(l1mb, axis=-1, keepdims=True)
        later = later_sc[...]
        laters = []
        for pg in range(n):
            laters.append(later)
            later = later + page_sum[pg * rows:(pg + 1) * rows]
        after = _suffix_sum(l1mb, u) + jnp.concatenate(laters, axis=0)
        a = jnp.exp(l1mb + z + after)
        if valid is not None:
            a = jnp.where(valid, a, 0.0)
        ab = a.astype(BF16)
        pv = []
        for h in range(n_heads):
            tot = None
            for pg in range(n):
                r0 = pg * rows + h * SUBLANES
                part = jnp.dot(ab[r0:r0 + SUBLANES], v_of(pg, h).astype(BF16), preferred_element_type=F32)
                tot = part if tot is None else tot + part
            pv.append(tot)
        acc_sc[...] += jnp.concatenate(pv, axis=0)
        later_sc[...] = later

    @pl.when(j == 0)
    def _():
        later_sc[...] = jnp.zeros((rows, 1), F32)
        acc_sc[...] = jnp.zeros((rows, LANES), F32)
        col, t = _new_token_mask(rows)
        pages(lambda pg, h: kn_ref[h], lambda pg, h: vn_ref[h], 1, col < t)

    pages(lambda pg, h: k_refs[pg][h], lambda pg, h: v_refs[pg][h], n_slots, None)

    @pl.when(j == n_steps - 1)
    def _():
        o_ref[...] = acc_sc[...].reshape(n_heads, SUBLANES, LANES).astype(o_ref.dtype)


def _sb_decode(page_table, q, k_new, v_new, cache_k, cache_v, layer, n_slots):
    nb, n_pages = page_table.shape
    n_heads = q.shape[1]
    n_steps = n_pages // n_slots
    page_blk = (n_heads, LANES, LANES)
    page_of = lambda j, p: n_pages - 1 - (j * n_slots + p)
    per_b = lambda b, j, pt: (b, 0, 0, 0)
    grid_spec = pltpu.PrefetchScalarGridSpec(
        num_scalar_prefetch=1,
        grid=(nb, n_steps),
        in_specs=[pl.BlockSpec((None, n_heads, SUBLANES, LANES), per_b),
                  pl.BlockSpec((None,) + page_blk, per_b),
                  pl.BlockSpec((None,) + page_blk, per_b)]
                 + _page_specs(page_blk, layer, n_slots, page_of) + _page_specs(page_blk, layer, n_slots, page_of),
        out_specs=pl.BlockSpec((None, n_heads, SUBLANES, LANES), per_b),
        scratch_shapes=[pltpu.VMEM((n_heads * SUBLANES, 1), F32), pltpu.VMEM((n_heads * SUBLANES, LANES), F32)])
    return pl.pallas_call(
        functools.partial(_sb_dec_kernel, n_slots=n_slots, n_heads=n_heads, n_steps=n_steps),
        grid_spec=grid_spec,
        out_shape=jax.ShapeDtypeStruct((nb, n_heads, SUBLANES, LANES), BF16),
        compiler_params=_cparams(2, 48),
        name="sb_decode_attn",
    )(page_table, q, k_new, v_new, *([cache_k] * n_slots), *([cache_v] * n_slots))


def _merge_kernel(oa, ob, oc, wa, wb, wc, g0, g1, g2, o_ref):
    pa = jnp.dot(oa[...], wa[...].astype(BF16), preferred_element_type=F32)
    pb = jnp.dot(ob[...], wb[...].astype(BF16), preferred_element_type=F32)
    pc = jnp.dot(oc[...], wc[...].astype(BF16), preferred_element_type=F32)
    o_ref[...] = (g0[...] * pa + g1[...] * pb + g2[...] * pc).astype(o_ref.dtype)


def _merge(oa, ob, oc, wa, wb, wc, layer, gates, tm, tn):
    m = oa.shape[0]
    d_model = wa.shape[2]
    nj = d_model // tn
    a_spec = lambda a: pl.BlockSpec((tm, a.shape[1]), lambda i, j: (i, 0))
    w_spec = lambda w: pl.BlockSpec((None, w.shape[1], tn), lambda i, j: (layer, 0, j))
    g_spec = lambda c: pl.BlockSpec((tm, tn), lambda i, j: (i, c * nj + j))
    return pl.pallas_call(
        _merge_kernel,
        grid=(m // tm, nj),
        in_specs=[a_spec(oa), a_spec(ob), a_spec(oc), w_spec(wa), w_spec(wb), w_spec(wc),
                  g_spec(0), g_spec(1), g_spec(2)],
        out_specs=pl.BlockSpec((tm, tn), lambda i, j: (i, j)),
        out_shape=jax.ShapeDtypeStruct((m, d_model), BF16),
        compiler_params=_cparams(2, 48),
        name="branch_merge",
    )(oa, ob, oc, wa, wb, wc, gates, gates, gates)


def _project(h, g, w_t, layer, name, col0, ncols, tn, epilogue, extras=(), extra_specs=(), heads=None, dtype=F32,
             stack=None, into=None):
    m, d_model = h.shape
    if heads is not None:
        shape, spec = _heads_out(g, heads, tn // LANES, dtype, stack)
    else:
        shape = jax.ShapeDtypeStruct((m, ncols), dtype)
        spec = pl.BlockSpec((g.tm, tn), lambda i, j, k: (i, j))
    return _mm(name, h, w_t, layer, col0, ncols, g.tm, tn, d_model, epilogue, list(extras), list(extra_specs),
               shape, spec, w_rows_are_outputs=True, into=into)


def _vec_spec(n):
    return pl.BlockSpec((1, n), lambda i, j, k: (0, 0))


def _layer_front(x, mod, g, p, dims, stack=None, prev=None):
    fox_w, dqk_w, dv_w, sb_w, d_model = dims
    kv = lambda name: dict(stack=stack, into=None if prev is None else prev[name])
    fh, sh = fox_w // LANES, sb_w // LANES
    h = _prenorm(x, p["g_norm1"], mod, g, 1, 0)
    tn = 512
    sc128 = 1.0 / math.sqrt(LANES)
    rms = lambda scale: functools.partial(_epi_heads_rms, g=g, scale=scale)
    plain = lambda scale: functools.partial(_epi_heads_plain, g=g, scale=scale)
    half = lambda scale: functools.partial(_epi_rms_half, scale=scale)
    out = {}
    proj = functools.partial(_project, h, g, p["w_in_t"], p["layer"])
    c = 0
    out["qa"] = proj("proj_fox_q", c, fox_w, tn, rms(sc128), [p["g_q_fox"]], [_vec_spec(LANES)], heads=fh, dtype=BF16); c += fox_w
    out["ka"] = proj("proj_fox_k", c, fox_w, tn, rms(1.0), [p["g_k_fox"]], [_vec_spec(LANES)], heads=fh, **kv("ka")); c += fox_w
    out["va"] = proj("proj_fox_v", c, fox_w, tn, plain(1.0), heads=fh, **kv("va")); c += fox_w
    out["qb"] = proj("proj_diff_q", c, dqk_w, tn, half(0.125), [p["g_q_diff"]], [_vec_spec(LANES)], dtype=BF16); c += dqk_w
    out["kb"] = proj("proj_diff_k", c, dqk_w, tn, half(1.0), [p["g_k_diff"]], [_vec_spec(LANES)]); c += dqk_w
    out["vb"] = proj("proj_diff_v", c, dv_w, tn, _epi_plain2d); c += dv_w
    out["qc"] = proj("proj_sb_q", c, sb_w, tn, plain(sc128), heads=sh, dtype=BF16); c += sb_w
    out["kc"] = proj("proj_sb_k", c, sb_w, tn, plain(1.0), heads=sh, **kv("kc")); c += sb_w
    out["vc"] = proj("proj_sb_v", c, sb_w, tn, plain(1.0), heads=sh, **kv("vc")); c += sb_w
    out["gates"] = proj("proj_gates", c, 3 * d_model, _pick(3 * d_model, 1024), _epi_sigmoid, dtype=BF16)
    out["lf_t"] = _lf_proj(p["w_fa_t"], h, p["b_fa"])
    return out


def _layer_back(x, mod, g, p, oa, ob, oc, gates):
    m, d_model = x.shape
    layer = p["layer"]
    merged = _merge(oa, ob, oc, p["w_br_fox"], p["w_br_diff"], p["w_br_sb"], layer, gates, g.tm, 256)
    tn = 512
    xspec = pl.BlockSpec((g.tm, tn), lambda i, j, k: (i, j))
    oshape = jax.ShapeDtypeStruct((m, d_model), F32)
    x1 = _mm("out_proj", merged, p["w_out"], layer, 0, d_model, g.tm, tn, d_model, _epi_residual,
             [x, mod], [xspec, _mod_spec(g, 2, d_model, tn)], oshape, xspec)
    h2 = _prenorm(x1, p["g_norm2"], mod, g, 4, 3)
    d_ff = p["w_up"].shape[2]
    f = _mm("mlp_up", h2, p["w_up"], layer, 0, d_ff, g.tm, tn, d_model, _epi_relu2, [], [],
            jax.ShapeDtypeStruct((m, d_ff), BF16), xspec)
    tn2 = _pick(d_model, 1024)
    xspec2 = pl.BlockSpec((g.tm, tn2), lambda i, j, k: (i, j))
    return _mm("mlp_down", f, p["w_down"], layer, 0, d_model, g.tm, tn2, _pick(d_ff, 2048), _epi_residual,
               [x1, mod], [xspec2, _mod_spec(g, 5, d_model, tn2)], oshape, xspec2)


def kernel(x_prompt, x_sample, cache_fox_k, cache_fox_v, cache_fox_logf, cache_diff_k, cache_diff_v, cache_sb_k, cache_sb_v, page_table, c_prompt, c_sample, w_ada, b_ada, g_norm1, w_in, b_fgate, g_q_fox, g_k_fox, g_q_diff, g_k_diff, lambda_q1, lambda_k1, lambda_q2, lambda_k2, g_sub_diff, w_br_fox, w_br_diff, w_br_sb, w_out, g_norm2, w_up, w_down):
    n_layers = w_ada.shape[0]
    nbp, seq, d_model = x_prompt.shape
    nbs, dseq, _ = x_sample.shape
    assert dseq == SUBLANES, "decode kernels hold the new tokens of a sequence in one sublane group"
    n_pages = page_table.shape[1]
    n_pool, page_size, fh, fd = cache_fox_k.shape[1:]
    dh, dqk = cache_diff_k.shape[3], cache_diff_k.shape[5]
    sh = cache_sb_k.shape[3]
    assert page_size == LANES and fd == LANES and 2 * dqk == LANES and cache_sb_k.shape[4] == LANES
    fox_w, dqk_w, dv_w, sb_w = fh * LANES, dh * LANES, dh * LANES, sh * LANES
    dims = (fox_w, dqk_w, dv_w, sb_w, d_model)
    past = n_pages * page_size
    mp, ms = nbp * seq, nbs * dseq

    gp = Group(nbp, seq, _pick(seq, 1024), False)
    gs = Group(nbs, dseq, ms, True)
    tq_fox, tq_diff, tq_sb = _pick(seq, 1024), _pick(seq, 512), _pick(seq, 512)
    tk_fox, tk_diff, tk_sb = _pick(seq, 512), _pick(seq, 512), _pick(seq, 256)
    n_slots = max(s for s in (8, 4, 2, 1) if n_pages % s == 0)

    n_c = nbp + nbs
    c_rows = -(-n_c // SUBLANES) * SUBLANES
    c_all = jnp.concatenate([c_prompt, c_sample, jnp.zeros((c_rows - n_c, d_model), F32)], axis=0)
    mod_all = _ada(c_all, w_ada, b_ada)

    fox_k_hm = jnp.transpose(cache_fox_k, (0, 1, 3, 2, 4))
    fox_v_hm = jnp.transpose(cache_fox_v, (0, 1, 3, 2, 4))
    sb_k_hm = jnp.transpose(cache_sb_k, (0, 1, 3, 2, 4))
    sb_v_hm = jnp.transpose(cache_sb_v, (0, 1, 3, 2, 4))
    diff_kt = jnp.transpose(cache_diff_k, (0, 1, 3, 4, 5, 2)).reshape(n_layers, n_pool, dh, LANES, page_size)
    logf_t = jnp.transpose(cache_fox_logf, (0, 3, 1, 2))
    pool_pad = -(-n_pool // SUBLANES) * SUBLANES

    c_fa = 3 * fox_w
    n_fa = fh
    yp, ys = x_prompt.reshape(mp, d_model), x_sample.reshape(ms, d_model)
    outs_p, outs_s = [], []
    w_fa_rows = jnp.transpose(w_in, (2, 0, 1))[c_fa:c_fa + n_fa]
    w_in_l = _w_in_by_layer(w_in, c_fa, n_fa, 256)
    w_down_bf = w_down.astype(BF16)
    for l in range(n_layers):
        lam_init = 0.8 - 0.6 * math.exp(-0.3 * l)
        pick_l = (jnp.arange(n_layers) == l)[None, :, None]
        p = {
            "layer": l,
            "w_in_t": w_in_l,
            "w_fa_t": jnp.pad(jnp.sum(jnp.where(pick_l, w_fa_rows, 0.0), axis=1), ((0, HEAD_PAD - n_fa), (0, 0))),
            "b_fa": jnp.pad(b_fgate[l], (0, HEAD_PAD - n_fa)).reshape(HEAD_PAD, 1),
            "g_norm1": g_norm1[l], "g_norm2": g_norm2[l],
            "g_q_fox": g_q_fox[l].reshape(1, LANES), "g_k_fox": g_k_fox[l].reshape(1, LANES),
            "g_q_diff": jnp.tile(g_q_diff[l], 2).reshape(1, LANES), "g_k_diff": jnp.tile(g_k_diff[l], 2).reshape(1, LANES),
            "w_br_fox": w_br_fox, "w_br_diff": w_br_diff, "w_br_sb": w_br_sb,
            "w_out": w_out, "w_up": w_up, "w_down": w_down_bf,
        }
        lams = [v[l].reshape(1, dqk) for v in (lambda_q1, lambda_k1, lambda_q2, lambda_k2)]
        g_sub = g_sub_diff[l].reshape(1, LANES)

        mod_p = mod_all[l, :nbp].reshape(nbp, 1, -1)
        fr = _layer_front(yp, mod_p, gp, p, dims, stack=(n_layers, l), prev=fr if l else None)
        lf_p = fr["lf_t"]
        x_e = lf_p.reshape(HEAD_PAD, nbp, seq // LANES, LANES).transpose(1, 2, 0, 3)
        e_p = _ecum(x_e).transpose(0, 2, 1, 3).reshape(nbp, HEAD_PAD, 1, seq)
        oa = _fox_prompt(fr["qa"], fr["ka"], fr["va"], l, e_p, tq_fox, tk_fox)
        ob = _diff_prompt(fr["qb"], fr["kb"], fr["vb"], lams, g_sub, nbp, seq, tq_diff, tk_diff, lam_init)
        oc = _sb_prompt(fr["qc"], fr["kc"], fr["vc"], l, tq_sb, tk_sb)
        yp = _layer_back(yp, mod_p, gp, p, oa, ob, oc, fr["gates"])
        hm_out = lambda a: jnp.transpose(a, (0, 2, 1, 3))
        outs_p.append((
            lf_p[:fh].reshape(fh, nbp, seq).transpose(1, 2, 0),
            fr["kb"].reshape(nbp, seq, dh, 2, dqk), fr["vb"].reshape(nbp, seq, dh, LANES)))

        mod_s = jnp.repeat(mod_all[l, nbp:nbp + nbs], dseq, axis=0)
        fs = _layer_front(ys, mod_s, gs, p, dims)
        lf_s = fs["lf_t"]
        lf_new = jnp.pad(lf_s.reshape(HEAD_PAD, nbs, dseq).transpose(1, 0, 2), ((0, 0), (0, 0), (0, LANES - dseq)))
        table = jnp.pad(logf_t[l], ((0, 0), (0, pool_pad - n_pool), (0, 0))).reshape(fh, pool_pad // SUBLANES, SUBLANES, LANES)
        e_s = _ecum(_lf_gather(page_table, table, lf_new))
        pad_keys = lambda a: jnp.pad(a, ((0, 0), (0, 0), (0, LANES - dseq), (0, 0)))
        oa = _fox_decode(page_table, fs["qa"], e_s, pad_keys(fs["ka"]), pad_keys(fs["va"]), fox_k_hm, fox_v_hm, l, n_slots)
        qb_hm = fs["qb"].reshape(nbs, dseq, dh, LANES).transpose(0, 2, 1, 3)
        kb_t = fs["kb"].reshape(nbs, dseq, dh, LANES).transpose(0, 2, 3, 1)
        kt_new = jnp.pad(kb_t, ((0, 0), (0, 0), (0, 0), (0, LANES - dseq)))
        vb_new = jnp.pad(fs["vb"].reshape(nbs, dseq, dh, LANES), ((0, 0), (0, LANES - dseq), (0, 0), (0, 0)))
        ob = _diff_decode(page_table, qb_hm, kt_new, vb_new, lams, g_sub, diff_kt, cache_diff_v, l, n_slots, lam_init)
        oc = _sb_decode(page_table, fs["qc"], pad_keys(fs["kc"]), pad_keys(fs["vc"]), sb_k_hm, sb_v_hm, l, n_slots)
        rows2d = lambda a: jnp.transpose(a, (0, 2, 1, 3)).reshape(ms, -1)
        ys = _layer_back(ys, mod_s, gs, p, rows2d(oa), rows2d(ob), rows2d(oc), fs["gates"])
        outs_s.append((
            hm_out(fs["ka"]), hm_out(fs["va"]), lf_s[:fh].reshape(fh, nbs, dseq).transpose(1, 2, 0),
            fs["kb"].reshape(nbs, dseq, dh, 2, dqk), fs["vb"].reshape(nbs, dseq, dh, LANES),
            hm_out(fs["kc"]), hm_out(fs["vc"])))

    stack = lambda outs: tuple(jnp.stack([o[i] for o in outs]) for i in range(len(outs[0])))
    p_lf, p_dk, p_dv = stack(outs_p)
    hm_stacked = lambda a: jnp.transpose(a, (0, 1, 3, 2, 4))
    prompt_rows = (hm_stacked(fr["ka"]), hm_stacked(fr["va"]), p_lf, p_dk, p_dv, hm_stacked(fr["kc"]), hm_stacked(fr["vc"]))
    return (yp.reshape(nbp, seq, d_model), ys.reshape(nbs, dseq, d_model)) + prompt_rows + stack(outs_s)
```

```python
import functools
import math
from typing import NamedTuple

import jax
import jax.numpy as jnp
from jax import lax
from jax.experimental import pallas as pl
from jax.experimental.pallas import tpu as pltpu

F32 = jnp.float32
BF16 = jnp.bfloat16
NORM_EPS = 1e-6
LANES = 128
SUBLANES = 8
HEAD_PAD = 16
NT_DIMS = (((1,), (1,)), ((), ()))
MIB = 1 << 20
NEG_INF = float("-inf")


class Group(NamedTuple):
    B: int
    T: int
    tm: int
    per_row_mod: bool


def _pick(dim, pref):
    t = min(pref, dim)
    while t >= LANES:
        if dim % t == 0 and t % LANES == 0:
            return t
        t -= LANES
    return dim


def _cparams(n_axes, vmem_mib=None):
    return pltpu.CompilerParams(
        dimension_semantics=("arbitrary",) * n_axes,
        vmem_limit_bytes=None if vmem_mib is None else vmem_mib * MIB)


def _log_sigmoid(x):
    return jnp.minimum(x, 0.0) - jnp.log(1.0 + jnp.exp(-jnp.abs(x)))


def _split3(x):
    hi = x.astype(BF16)
    r1 = x - hi.astype(F32)
    mid = r1.astype(BF16)
    lo = (r1 - mid.astype(F32)).astype(BF16)
    return hi, mid, lo


def _suffix_sum(x, u):
    hi = x.astype(BF16)
    lo = (x - hi.astype(F32)).astype(BF16)
    return jnp.dot(hi, u, preferred_element_type=F32) + jnp.dot(lo, u, preferred_element_type=F32)


def _div_mod(x, n):
    if n & (n - 1) == 0:
        return lax.shift_right_logical(x, n.bit_length() - 1), jnp.bitwise_and(x, n - 1)
    return lax.div(x, n), lax.rem(x, n)


def _suffix_ones(n):
    r = lax.broadcasted_iota(jnp.int32, (n, n), 0)
    c = lax.broadcasted_iota(jnp.int32, (n, n), 1)
    return (r > c).astype(BF16)


def _mod_spec(g, chunk, d_model, tn):
    nj = d_model // tn
    if g.per_row_mod:
        return pl.BlockSpec((g.tm, tn), lambda i, j, *_: (i, chunk * nj + j))
    nt = g.T // g.tm
    return pl.BlockSpec((None, 1, tn), lambda i, j, *_: (i // nt, 0, chunk * nj + j))


def _ada_kernel(c_ref, w_ref, b_ref, o_ref):
    c = c_ref[...]
    a = (c * jax.nn.sigmoid(c)).astype(BF16)
    o_ref[...] = jnp.dot(a, w_ref[...].astype(BF16), preferred_element_type=F32) + b_ref[...]


def _ada(c_all, w_ada, b_ada):
    n_layers, d_model, n = w_ada.shape
    rows = c_all.shape[0]
    tn = _pick(n, 512)
    return pl.pallas_call(
        _ada_kernel,
        grid=(n_layers, n // tn),
        in_specs=[pl.BlockSpec((rows, d_model), lambda l, j: (0, 0)),
                  pl.BlockSpec((None, d_model, tn), lambda l, j: (l, 0, j)),
                  pl.BlockSpec((None, 1, tn), lambda l, j: (l, 0, j))],
        out_specs=pl.BlockSpec((None, rows, tn), lambda l, j: (l, 0, j)),
        out_shape=jax.ShapeDtypeStruct((n_layers, rows, n), F32),
        compiler_params=_cparams(2, 40),
        name="ada_modulation",
    )(c_all, w_ada, b_ada.reshape(n_layers, 1, n))


def _sublane_transpose(tiles):
    tiles = list(tiles)
    axis = tiles[0].ndim - 2
    sub = lax.broadcasted_iota(jnp.int32, tiles[0].shape, axis)
    for d in (4, 2, 1):
        keep_low = jnp.bitwise_and(sub, d) == 0
        for a in range(SUBLANES):
            if a & d:
                continue
            lo, hi = tiles[a], tiles[a + d]
            tiles[a] = jnp.where(keep_low, lo, pltpu.roll(hi, d, axis))
            tiles[a + d] = jnp.where(keep_low, pltpu.roll(lo, SUBLANES - d, axis), hi)
    return tiles


def _w_in_split_kernel(w_ref, o_ref, *, tn, n_layers, n_chunks):
    per_col = n_chunks * n_layers
    groups = per_col // SUBLANES
    x = w_ref[...].reshape(tn // SUBLANES, SUBLANES, groups, SUBLANES, LANES)
    by_row = _sublane_transpose([x[:, i] for i in range(SUBLANES)])
    for c_lo in range(SUBLANES):
        for c_hi in range(groups):
            kc, l = divmod(c_hi * SUBLANES + c_lo, n_layers)
            rows = by_row[c_lo][:, c_hi].reshape(tn, LANES)
            o_ref[l, :, kc * LANES:(kc + 1) * LANES] = rows.astype(o_ref.dtype)


def _w_in_by_layer(w_in, skip0, n_skip, tn):
    n_layers, kdim, n = w_in.shape
    n_chunks = kdim // LANES
    per_col = n_chunks * n_layers
    rows = w_in.reshape(n_layers, n_chunks, LANES, n).transpose(3, 1, 0, 2).reshape(n * per_col, LANES)
    n_out = n - n_skip
    assert skip0 % tn == 0 and n_out % tn == 0

    def src(i):
        col = i * tn
        return (jnp.where(col >= skip0, col + n_skip, col) * per_col, 0)

    return pl.pallas_call(
        functools.partial(_w_in_split_kernel, tn=tn, n_layers=n_layers, n_chunks=n_chunks),
        grid=(n_out // tn,),
        in_specs=[pl.BlockSpec((pl.Element(tn * per_col), pl.Element(LANES)), src)],
        out_specs=pl.BlockSpec((n_layers, tn, kdim), lambda i: (0, i, 0)),
        out_shape=jax.ShapeDtypeStruct((n_layers, n_out, kdim), BF16),
        compiler_params=_cparams(1, 40),
        name="w_in_by_layer",
    )(rows)


def _prenorm_kernel(x_ref, g_ref, sc_ref, sh_ref, o_ref):
    x = x_ref[...]
    ms = jnp.mean(x * x, axis=-1, keepdims=True)
    y = x * lax.rsqrt(ms + NORM_EPS) * g_ref[...]
    o_ref[...] = (y * (1.0 + sc_ref[...]) + sh_ref[...]).astype(o_ref.dtype)


def _prenorm(x, gain, mod, g, scale_chunk, shift_chunk):
    m, d_model = x.shape
    tm = min(g.tm, 256)
    gg = g._replace(tm=tm)
    return pl.pallas_call(
        _prenorm_kernel,
        grid=(m // tm, 1),
        in_specs=[pl.BlockSpec((tm, d_model), lambda i, j: (i, 0)),
                  pl.BlockSpec((1, d_model), lambda i, j: (0, 0)),
                  _mod_spec(gg, scale_chunk, d_model, d_model),
                  _mod_spec(gg, shift_chunk, d_model, d_model)],
        out_specs=pl.BlockSpec((tm, d_model), lambda i, j: (i, 0)),
        out_shape=jax.ShapeDtypeStruct((m, d_model), BF16),
        compiler_params=_cparams(2, 40),
        name="mod_rmsnorm",
    )(x, gain.reshape(1, d_model), mod, mod)


def _mm_kernel(*refs, nk, n_extra, n_aliased, epilogue, w_rows_are_outputs):
    a_ref, w_ref = refs[0], refs[1]
    extras = refs[2:2 + n_extra]
    refs = refs[:2 + n_extra] + refs[2 + n_extra + n_aliased:]
    if w_rows_are_outputs:
        part = lax.dot_general(a_ref[...], w_ref[...].astype(BF16), NT_DIMS, preferred_element_type=F32)
    else:
        part = jnp.dot(a_ref[...], w_ref[...].astype(BF16), preferred_element_type=F32)
    if nk == 1:
        epilogue(part, extras, refs[2 + n_extra:])
        return
    outs, acc_ref = refs[2 + n_extra:-1], refs[-1]
    k = pl.program_id(2)

    @pl.when(k == 0)
    def _():
        acc_ref[...] = part

    @pl.when(k > 0)
    def _():
        acc_ref[...] += part

    @pl.when(k == nk - 1)
    def _():
        epilogue(acc_ref[...], extras, outs)


def _mm(name, a, w, layer, col0, ncols, tm, tn, tk, epilogue, extras, extra_specs, out_shape, out_specs,
        w_rows_are_outputs=False, into=None, vmem_mib=48):
    m, kdim = a.shape
    assert col0 % tn == 0 and ncols % tn == 0 and m % tm == 0 and kdim % tk == 0
    nk, c0 = kdim // tk, col0 // tn
    if w_rows_are_outputs:
        w_spec = pl.BlockSpec((None, tn, tk), lambda i, j, k: (layer, c0 + j, k))
    else:
        w_spec = pl.BlockSpec((None, tk, tn), lambda i, j, k: (layer, k, c0 + j))
    aliased = [] if into is None else [into]
    kern = functools.partial(_mm_kernel, nk=nk, n_extra=len(extras), n_aliased=len(aliased), epilogue=epilogue,
                             w_rows_are_outputs=w_rows_are_outputs)
    return pl.pallas_call(
        kern,
        grid=(m // tm, ncols // tn, nk),
        in_specs=[pl.BlockSpec((tm, tk), lambda i, j, k: (i, k)), w_spec] + list(extra_specs)
                 + [pl.BlockSpec(memory_space=pl.ANY)] * len(aliased),
        out_specs=out_specs,
        out_shape=out_shape,
        input_output_aliases={2 + len(extras): 0} if aliased else {},
        scratch_shapes=[pltpu.VMEM((tm, tn), F32)] if nk > 1 else [],
        compiler_params=_cparams(3, vmem_mib),
        name=name,
    )(a, w, *extras, *aliased)


def _store_heads(val, o_ref, g):
    for h in range(val.shape[1] // LANES):
        piece = val[:, h * LANES:(h + 1) * LANES].astype(o_ref.dtype)
        if g.per_row_mod:
            o_ref[:, h] = piece.reshape(g.B, g.T, LANES)
        else:
            o_ref[h] = piece


def _heads_out(g, n_heads, nh_tile, dtype, stack=None):
    shape = jax.ShapeDtypeStruct((g.B, n_heads, g.T, LANES), dtype)
    if g.per_row_mod:
        assert stack is None
        return shape, pl.BlockSpec((g.B, nh_tile, g.T, LANES), lambda i, j, k: (0, j, 0, 0))
    nt = g.T // g.tm
    if stack is None:
        return shape, pl.BlockSpec((None, nh_tile, g.tm, LANES), lambda i, j, k: (i // nt, j, i % nt, 0))
    n_layers, layer = stack
    shape = jax.ShapeDtypeStruct((n_layers,) + shape.shape, dtype)
    return shape, pl.BlockSpec((None, None, nh_tile, g.tm, LANES), lambda i, j, k: (layer, i // nt, j, i % nt, 0))


def _epi_heads_plain(acc, extras, outs, *, g, scale):
    _store_heads(acc if scale == 1.0 else acc * scale, outs[0], g)


def _epi_heads_rms(acc, extras, outs, *, g, scale):
    gain = extras[0][...]
    pieces = []
    for h in range(acc.shape[1] // LANES):
        x = acc[:, h * LANES:(h + 1) * LANES]
        y = x * lax.rsqrt(jnp.mean(x * x, axis=-1, keepdims=True) + NORM_EPS) * gain
        pieces.append(y if scale == 1.0 else y * scale)
    _store_heads(jnp.concatenate(pieces, axis=1), outs[0], g)


def _epi_rms_half(acc, extras, outs, *, scale):
    gain = extras[0][...]
    lo = lax.broadcasted_iota(jnp.int32, (1, LANES), 1) < (LANES // 2)
    pieces = []
    for h in range(acc.shape[1] // LANES):
        x = acc[:, h * LANES:(h + 1) * LANES]
        sq = x * x
        s_lo = jnp.sum(jnp.where(lo, sq, 0.0), axis=-1, keepdims=True)
        s_hi = jnp.sum(jnp.where(lo, 0.0, sq), axis=-1, keepdims=True)
        ms = jnp.where(lo, s_lo, s_hi) * (2.0 / LANES)
        y = x * lax.rsqrt(ms + NORM_EPS) * gain
        pieces.append(y if scale == 1.0 else y * scale)
    outs[0][...] = jnp.concatenate(pieces, axis=1).astype(outs[0].dtype)


def _epi_plain2d(acc, extras, outs):
    outs[0][...] = acc.astype(outs[0].dtype)


def _epi_sigmoid(acc, extras, outs):
    outs[0][...] = jax.nn.sigmoid(acc).astype(outs[0].dtype)


def _epi_relu2(acc, extras, outs):
    r = jnp.maximum(acc, 0.0)
    outs[0][...] = (r * r).astype(outs[0].dtype)


def _epi_residual(acc, extras, outs):
    x_ref, gate_ref = extras
    outs[0][...] = x_ref[...] + gate_ref[...] * acc


def _lf_kernel(w_ref, h_ref, b_ref, o_ref):
    u = lax.dot_general(w_ref[...].astype(BF16), h_ref[...], NT_DIMS, preferred_element_type=F32)
    o_ref[...] = _log_sigmoid(u + b_ref[...])


def _lf_proj(w_fa_t, h, b_fa):
    m, d_model = h.shape
    tm = _pick(m, 512)
    return pl.pallas_call(
        _lf_kernel,
        grid=(m // tm,),
        in_specs=[pl.BlockSpec((HEAD_PAD, d_model), lambda i: (0, 0)),
                  pl.BlockSpec((tm, d_model), lambda i: (i, 0)),
                  pl.BlockSpec((HEAD_PAD, 1), lambda i: (0, 0))],
        out_specs=pl.BlockSpec((HEAD_PAD, tm), lambda i: (0, i)),
        out_shape=jax.ShapeDtypeStruct((HEAD_PAD, m), F32),
        compiler_params=_cparams(1, 32),
        name="proj_logf",
    )(w_fa_t, h, b_fa)


def _ecum_kernel(x_ref, o_ref, loc_sc, tot_sc, *, n_chunks):
    x = x_ref[...].reshape(n_chunks * HEAD_PAD, LANES)
    u = _suffix_ones(LANES)
    hi, mid, lo = _split3(x)
    loc = (jnp.dot(hi, u, preferred_element_type=F32) + jnp.dot(mid, u, preferred_element_type=F32)
           + jnp.dot(lo, u, preferred_element_type=F32))
    loc_sc[...] = loc.reshape(n_chunks, HEAD_PAD, LANES)
    tot_sc[...] = jnp.sum(x, axis=-1, keepdims=True).reshape(n_chunks, HEAD_PAD, 1)

    def body(i, later):
        j = n_chunks - 1 - i
        o_ref[j] = loc_sc[j] + later
        return later + tot_sc[j]

    lax.fori_loop(0, n_chunks, body, jnp.zeros((HEAD_PAD, 1), F32))


def _ecum(x):
    nb, n_chunks = x.shape[0], x.shape[1]
    return pl.pallas_call(
        functools.partial(_ecum_kernel, n_chunks=n_chunks),
        grid=(nb,),
        in_specs=[pl.BlockSpec((None, n_chunks, HEAD_PAD, LANES), lambda b: (b, 0, 0, 0))],
        out_specs=pl.BlockSpec((None, n_chunks, HEAD_PAD, LANES), lambda b: (b, 0, 0, 0)),
        out_shape=jax.ShapeDtypeStruct(x.shape, F32),
        scratch_shapes=[pltpu.VMEM((n_chunks, HEAD_PAD, LANES), F32), pltpu.VMEM((n_chunks, HEAD_PAD, 1), F32)],
        compiler_params=_cparams(1, 32),
        name="logf_suffix_sum",
    )(x)


def _lf_gather_kernel(pt_ref, tab_ref, new_ref, o_ref, *, n_pages, n_heads):
    b = pl.program_id(0)
    sub = lax.broadcasted_iota(jnp.int32, (SUBLANES, LANES), 0)

    def body(j, carry):
        page = pt_ref[b, j]
        blk = lax.shift_right_logical(page, 3)
        row = jnp.bitwise_and(page, SUBLANES - 1)
        for h in range(n_heads):
            tile = tab_ref[h, blk]
            o_ref[j, h:h + 1, :] = jnp.sum(jnp.where(sub == row, tile, 0.0), axis=0, keepdims=True)
        o_ref[j, n_heads:HEAD_PAD, :] = jnp.zeros((HEAD_PAD - n_heads, LANES), F32)
        return carry

    lax.fori_loop(0, n_pages, body, 0)
    o_ref[n_pages] = new_ref[...]


def _lf_gather(page_table, table, lf_new):
    nb, n_pages = page_table.shape
    n_heads = table.shape[0]
    grid_spec = pltpu.PrefetchScalarGridSpec(
        num_scalar_prefetch=1,
        grid=(nb,),
        in_specs=[pl.BlockSpec(table.shape, lambda b, pt: (0, 0, 0, 0)),
                  pl.BlockSpec((None, HEAD_PAD, LANES), lambda b, pt: (b, 0, 0))],
        out_specs=pl.BlockSpec((None, n_pages + 1, HEAD_PAD, LANES), lambda b, pt: (b, 0, 0, 0)))
    return pl.pallas_call(
        functools.partial(_lf_gather_kernel, n_pages=n_pages, n_heads=n_heads),
        grid_spec=grid_spec,
        out_shape=jax.ShapeDtypeStruct((nb, n_pages + 1, HEAD_PAD, LANES), F32),
        compiler_params=_cparams(1, 40),
        name="logf_page_gather",
    )(page_table, table, lf_new)


def _softmax_step(s, v, carry):
    m, l, acc = carry
    m_new = jnp.maximum(m, jnp.max(s, axis=-1, keepdims=True))
    alpha = jnp.exp(m - m_new)
    p = jnp.exp(s - m_new)
    l = alpha * l + jnp.sum(p, axis=-1, keepdims=True)
    acc = alpha * acc + jnp.dot(p.astype(BF16), v, preferred_element_type=F32)
    return m_new, l, acc


def _causal_blocks(qi, tq, tk):
    return (qi * tq) // tk, -(-((qi + 1) * tq) // tk)


def _fox_prompt_kernel(q_ref, k_ref, v_ref, e_ref, o_ref, *, seq, tq, tk):
    for qi in range(seq // tq):
        q = q_ref[pl.ds(qi * tq, tq), :]

        def step(ki, carry, masked, q=q, qi=qi):
            k0 = ki * tk
            k = k_ref[pl.ds(k0, tk), :].astype(BF16)
            v = v_ref[pl.ds(k0, tk), :].astype(BF16)
            s = lax.dot_general(q, k, NT_DIMS, preferred_element_type=F32) + e_ref[:, pl.ds(k0, tk)]
            if masked:
                row = lax.broadcasted_iota(jnp.int32, (tq, tk), 0) + qi * tq
                col = lax.broadcasted_iota(jnp.int32, (tq, tk), 1) + k0
                s = jnp.where(col <= row, s, NEG_INF)
            return _softmax_step(s, v, carry)

        n_full, n_end = _causal_blocks(qi, tq, tk)
        carry = (jnp.full((tq, 1), NEG_INF, F32), jnp.zeros((tq, 1), F32), jnp.zeros((tq, LANES), F32))
        for kd in range(n_full, n_end):
            carry = step(kd, carry, True)
        for ki in range(n_full):
            carry = step(ki, carry, False)
        _, l, acc = carry
        o_ref[pl.ds(qi * tq, tq), :] = (acc / l).astype(o_ref.dtype)


def _kv_spec(kv, layer, seq):
    if kv.ndim == 4:
        return pl.BlockSpec((None, None, seq, LANES), lambda b, h: (b, h, 0, 0))
    return pl.BlockSpec((None, None, None, seq, LANES), lambda b, h: (layer, b, h, 0, 0))


def _fox_prompt(q, k, v, layer, e_rows, tq, tk):
    nb, nh, seq, _ = q.shape
    hm = lambda b, h: (b, h, 0, 0)
    return pl.pallas_call(
        functools.partial(_fox_prompt_kernel, seq=seq, tq=tq, tk=tk),
        grid=(nb, nh),
        in_specs=[pl.BlockSpec((None, None, seq, LANES), hm),
                  _kv_spec(k, layer, seq),
                  _kv_spec(v, layer, seq),
                  pl.BlockSpec((None, None, 1, seq), hm)],
        out_specs=pl.BlockSpec((seq, LANES), lambda b, h: (b, h)),
        out_shape=jax.ShapeDtypeStruct((nb * seq, nh * LANES), BF16),
        compiler_params=_cparams(2, 32),
        name="fox_prompt_attn",
    )(q, k, v, e_rows)


def _diff_lambda(lq1, lk1, lq2, lk2, lam_init):
    return (jnp.exp(jnp.sum(lq1[...] * lk1[...], axis=-1, keepdims=True))
            - jnp.exp(jnp.sum(lq2[...] * lk2[...], axis=-1, keepdims=True)) + lam_init)


def _diff_finish(acc0, l0, acc1, l1, lam, gain, lam_init):
    o = acc0 / l0 - lam * (acc1 / l1)
    o = o * lax.rsqrt(jnp.mean(o * o, axis=-1, keepdims=True) + NORM_EPS) * gain
    return o * (1.0 - lam_init)


def _diff_prompt_kernel(q_ref, k_ref, v_ref, lq1, lk1, lq2, lk2, gs_ref, o_ref, *, seq, tq, tk, lam_init, n_heads):
    h = pl.program_id(1)
    slope = jnp.exp2(-(8.0 / n_heads) * (h + 1).astype(F32) * jnp.ones((1, tk), F32))
    lam = _diff_lambda(lq1, lk1, lq2, lk2, lam_init)
    lo = lax.broadcasted_iota(jnp.int32, (1, LANES), 1) < (LANES // 2)
    for qi in range(seq // tq):
        q = q_ref[pl.ds(qi * tq, tq), :]
        zero = jnp.zeros_like(q)
        q2 = jnp.concatenate([jnp.where(lo, q, zero), jnp.where(lo, zero, q)], axis=0)

        def step(ki, carry, masked, q2=q2, qi=qi):
            k0 = ki * tk
            k = k_ref[pl.ds(k0, tk), :].astype(BF16)
            v = v_ref[pl.ds(k0, tk), :].astype(BF16)
            kpos = (lax.broadcasted_iota(jnp.int32, (1, tk), 1) + (k0 - qi * tq)).astype(F32)
            s = lax.dot_general(q2, k, NT_DIMS, preferred_element_type=F32) + slope * kpos
            if masked:
                row = lax.broadcasted_iota(jnp.int32, (2 * tq, tk), 0)
                row = jnp.where(row >= tq, row - tq, row) + qi * tq
                col = lax.broadcasted_iota(jnp.int32, (2 * tq, tk), 1) + k0
                s = jnp.where(col <= row, s, NEG_INF)
            return _softmax_step(s, v, carry)

        n_full, n_end = _causal_blocks(qi, tq, tk)
        carry = (jnp.full((2 * tq, 1), NEG_INF, F32), jnp.zeros((2 * tq, 1), F32), jnp.zeros((2 * tq, LANES), F32))
        for kd in range(n_full, n_end):
            carry = step(kd, carry, True)
        for ki in range(n_full):
            carry = step(ki, carry, False)
        _, l, acc = carry
        out = _diff_finish(acc[:tq], l[:tq], acc[tq:], l[tq:], lam, gs_ref[...], lam_init)
        o_ref[pl.ds(qi * tq, tq), :] = out.astype(o_ref.dtype)


def _diff_prompt(q, k, v, lams, g_sub, nb, seq, tq, tk, lam_init):
    nh = q.shape[1] // LANES
    blk = pl.BlockSpec((seq, LANES), lambda b, h: (b, h))
    vec64 = pl.BlockSpec((1, LANES // 2), lambda b, h: (0, 0))
    return pl.pallas_call(
        functools.partial(_diff_prompt_kernel, seq=seq, tq=tq, tk=tk, lam_init=lam_init, n_heads=nh),
        grid=(nb, nh),
        in_specs=[blk, blk, blk, vec64, vec64, vec64, vec64, pl.BlockSpec((1, LANES), lambda b, h: (0, 0))],
        out_specs=blk,
        out_shape=jax.ShapeDtypeStruct(q.shape, BF16),
        compiler_params=_cparams(2, 32),
        name="diff_prompt_attn",
    )(q, k, v, *lams, g_sub)


def _sb_block(z, v, later, acc, u, valid):
    l1mb = -jnp.maximum(z, 0.0) - jnp.log(1.0 + jnp.exp(-jnp.abs(z)))
    if valid is not None:
        l1mb = jnp.where(valid, l1mb, 0.0)
    after = _suffix_sum(l1mb, u) + later
    a = jnp.exp(l1mb + z + after)
    if valid is not None:
        a = jnp.where(valid, a, 0.0)
    acc = acc + jnp.dot(a.astype(BF16), v, preferred_element_type=F32)
    return later + jnp.sum(l1mb, axis=-1, keepdims=True), acc


def _sb_prompt_kernel(q_ref, k_ref, v_ref, o_ref, *, seq, tq, tk):
    u = _suffix_ones(tk)
    for qi in range(seq // tq):
        q = q_ref[pl.ds(qi * tq, tq), :]

        def step(ki, carry, masked, q=q, qi=qi):
            k0 = ki * tk
            k = k_ref[pl.ds(k0, tk), :].astype(BF16)
            v = v_ref[pl.ds(k0, tk), :].astype(BF16)
            z = lax.dot_general(q, k, NT_DIMS, preferred_element_type=F32)
            valid = None
            if masked:
                row = lax.broadcasted_iota(jnp.int32, (tq, tk), 0) + qi * tq
                col = lax.broadcasted_iota(jnp.int32, (tq, tk), 1) + k0
                valid = col < row
            return _sb_block(z, v, carry[0], carry[1], u, valid)

        n_full, n_end = _causal_blocks(qi, tq, tk)
        carry = (jnp.zeros((tq, 1), F32), jnp.zeros((tq, LANES), F32))
        for kd in reversed(range(n_full, n_end)):
            carry = step(kd, carry, True)
        for ki in reversed(range(n_full)):
            carry = step(ki, carry, False)
        o_ref[pl.ds(qi * tq, tq), :] = carry[1].astype(o_ref.dtype)


def _sb_prompt(q, k, v, layer, tq, tk):
    nb, nh, seq, _ = q.shape
    hm = lambda b, h: (b, h, 0, 0)
    return pl.pallas_call(
        functools.partial(_sb_prompt_kernel, seq=seq, tq=tq, tk=tk),
        grid=(nb, nh),
        in_specs=[pl.BlockSpec((None, None, seq, LANES), hm), _kv_spec(k, layer, seq), _kv_spec(v, layer, seq)],
        out_specs=pl.BlockSpec((seq, LANES), lambda b, h: (b, h)),
        out_shape=jax.ShapeDtypeStruct((nb * seq, nh * LANES), BF16),
        compiler_params=_cparams(2, 32),
        name="sb_prompt_attn",
    )(q, k, v)


def _page_specs(cache_block, layer, n_per_step, page_of):
    def spec(p):
        return pl.BlockSpec((None, None) + cache_block,
                            lambda b, j, pt: (layer, pt[b, page_of(j, p)]) + (0,) * len(cache_block))
    return [spec(p) for p in range(n_per_step)]


def _new_token_mask(rows):
    t = jnp.bitwise_and(lax.broadcasted_iota(jnp.int32, (rows, LANES), 0), SUBLANES - 1)
    return lax.broadcasted_iota(jnp.int32, (rows, LANES), 1), t


def _fox_dec_kernel(pt_ref, q_ref, e_ref, en_ref, kn_ref, vn_ref, *rest, n_slots, n_heads, n_steps):
    k_refs, v_refs = rest[:n_slots], rest[n_slots:2 * n_slots]
    o_ref, m_sc, l_sc, acc_sc = rest[2 * n_slots:]
    j = pl.program_id(1)
    rows = n_heads * SUBLANES

    @pl.when(j == 0)
    def _():
        m_sc[...] = jnp.full((rows, 1), NEG_INF, F32)
        l_sc[...] = jnp.zeros((rows, 1), F32)
        acc_sc[...] = jnp.zeros((rows, LANES), F32)

    def scores(k_of, e_of):
        return jnp.concatenate(
            [lax.dot_general(q_ref[h], k_of(h).astype(BF16), NT_DIMS, preferred_element_type=F32) + e_of(h)
             for h in range(n_heads)], axis=0)

    def update(s, v_of):
        m_prev = m_sc[...]
        m_new = jnp.maximum(m_prev, jnp.max(s, axis=-1, keepdims=True))
        alpha = jnp.exp(m_prev - m_new)
        p = jnp.exp(s - m_new)
        l_sc[...] = alpha * l_sc[...] + jnp.sum(p, axis=-1, keepdims=True)
        pb = p.astype(BF16)
        pv = []
        for h in range(n_heads):
            tot = None
            for pg in range(s.shape[1] // LANES):
                part = jnp.dot(pb[h * SUBLANES:(h + 1) * SUBLANES, pg * LANES:(pg + 1) * LANES],
                               v_of(pg, h).astype(BF16), preferred_element_type=F32)
                tot = part if tot is None else tot + part
            pv.append(tot)
        acc_sc[...] = alpha * acc_sc[...] + jnp.concatenate(pv, axis=0)
        m_sc[...] = m_new

    s_past = jnp.concatenate(
        [scores(lambda h, p=p: k_refs[p][h], lambda h, p=p: e_ref[p, h:h + 1, :]) for p in range(n_slots)], axis=1)
    update(s_past, lambda pg, h: v_refs[pg][h])

    @pl.when(j == n_steps - 1)
    def _():
        col, t = _new_token_mask(rows)
        s_new = scores(lambda h: kn_ref[h], lambda h: en_ref[h:h + 1, :])
        update(jnp.where(col <= t, s_new, NEG_INF), lambda pg, h: vn_ref[h])
        o_ref[...] = (acc_sc[...] / l_sc[...]).reshape(n_heads, SUBLANES, LANES).astype(o_ref.dtype)


def _fox_decode(page_table, q, e_all, k_new, v_new, cache_k, cache_v, layer, n_slots):
    nb, n_pages = page_table.shape
    n_heads = q.shape[1]
    n_steps = n_pages // n_slots
    page_blk = (n_heads, LANES, LANES)
    page_of = lambda j, p: j * n_slots + p
    per_b = lambda b, j, pt: (b, 0, 0, 0)
    grid_spec = pltpu.PrefetchScalarGridSpec(
        num_scalar_prefetch=1,
        grid=(nb, n_steps),
        in_specs=[pl.BlockSpec((None, n_heads, SUBLANES, LANES), per_b),
                  pl.BlockSpec((None, n_slots, HEAD_PAD, LANES), lambda b, j, pt: (b, j, 0, 0)),
                  pl.BlockSpec((None, None, HEAD_PAD, LANES), lambda b, j, pt: (b, n_pages, 0, 0)),
                  pl.BlockSpec((None,) + page_blk, per_b),
                  pl.BlockSpec((None,) + page_blk, per_b)]
                 + _page_specs(page_blk, layer, n_slots, page_of) + _page_specs(page_blk, layer, n_slots, page_of),
        out_specs=pl.BlockSpec((None, n_heads, SUBLANES, LANES), per_b),
        scratch_shapes=[pltpu.VMEM((n_heads * SUBLANES, 1), F32), pltpu.VMEM((n_heads * SUBLANES, 1), F32),
                        pltpu.VMEM((n_heads * SUBLANES, LANES), F32)])
    return pl.pallas_call(
        functools.partial(_fox_dec_kernel, n_slots=n_slots, n_heads=n_heads, n_steps=n_steps),
        grid_spec=grid_spec,
        out_shape=jax.ShapeDtypeStruct((nb, n_heads, SUBLANES, LANES), BF16),
        compiler_params=_cparams(2, 48),
        name="fox_decode_attn",
    )(page_table, q, e_all, e_all, k_new, v_new, *([cache_k] * n_slots), *([cache_v] * n_slots))


def _diff_dec_kernel(pt_ref, q_ref, kn_ref, vn_ref, lq1, lk1, lq2, lk2, gs_ref, *rest,
                     n_slots, n_heads, n_steps, past, lam_init):
    k_refs, v_refs = rest[:n_slots], rest[n_slots:2 * n_slots]
    o_ref, m_sc, l_sc, acc_sc = rest[2 * n_slots:]
    j = pl.program_id(1)
    hr = 2 * SUBLANES
    rows = n_heads * hr
    lo = lax.broadcasted_iota(jnp.int32, (1, LANES), 1) < (LANES // 2)

    @pl.when(j == 0)
    def _():
        m_sc[...] = jnp.full((rows, 1), NEG_INF, F32)
        l_sc[...] = jnp.zeros((rows, 1), F32)
        acc_sc[...] = jnp.zeros((rows, LANES), F32)

    col, t = _new_token_mask(hr)

    def scores(kt_of, first_pos):
        dist = ((past + t) - (first_pos + col)).astype(F32)
        out = []
        for h in range(n_heads):
            q = q_ref[h]
            zero = jnp.zeros_like(q)
            q2 = jnp.concatenate([jnp.where(lo, q, zero), jnp.where(lo, zero, q)], axis=0)
            s = jnp.dot(q2, kt_of(h).astype(BF16), preferred_element_type=F32)
            out.append(s - (2.0 ** (-(8.0 / n_heads) * (h + 1))) * dist)
        return jnp.concatenate(out, axis=0)

    ncol = LANES * n_heads
    slot_of_col, _ = _div_mod(lax.broadcasted_iota(jnp.int32, (LANES, ncol), 1), n_heads)
    spread = jnp.where(slot_of_col == lax.broadcasted_iota(jnp.int32, (LANES, ncol), 0), 1.0, 0.0).astype(BF16)
    head_of_row, _ = _div_mod(lax.broadcasted_iota(jnp.int32, (rows, ncol), 0), hr)
    own_head = head_of_row == _div_mod(lax.broadcasted_iota(jnp.int32, (rows, ncol), 1), n_heads)[1]

    def update(s, v_of):
        m_prev = m_sc[...]
        m_new = jnp.maximum(m_prev, jnp.max(s, axis=-1, keepdims=True))
        alpha = jnp.exp(m_prev - m_new)
        p = jnp.exp(s - m_new)
        l_sc[...] = alpha * l_sc[...] + jnp.sum(p, axis=-1, keepdims=True)
        pb = p.astype(BF16)
        tot = None
        for pg in range(s.shape[1] // LANES):
            wide = jnp.dot(pb[:, pg * LANES:(pg + 1) * LANES], spread, preferred_element_type=F32)
            wide = jnp.where(own_head, wide, 0.0).astype(BF16)
            part = jnp.dot(wide, v_of(pg).astype(BF16), preferred_element_type=F32)
            tot = part if tot is None else tot + part
        acc_sc[...] = alpha * acc_sc[...] + tot
        m_sc[...] = m_new

    s_past = jnp.concatenate(
        [scores(lambda h, p=p: k_refs[p][h], (j * n_slots + p) * LANES) for p in range(n_slots)], axis=1)
    update(s_past, lambda pg: v_refs[pg][...].reshape(ncol, LANES))

    @pl.when(j == n_steps - 1)
    def _():
        s_new = scores(lambda h: kn_ref[h], past)
        valid = jnp.concatenate([col <= t] * n_heads, axis=0)
        update(jnp.where(valid, s_new, NEG_INF), lambda pg: vn_ref[...].reshape(ncol, LANES))
        lam = _diff_lambda(lq1, lk1, lq2, lk2, lam_init)
        acc, l = acc_sc[...], l_sc[...]
        for h in range(n_heads):
            r0 = h * hr
            out = _diff_finish(acc[r0:r0 + SUBLANES], l[r0:r0 + SUBLANES], acc[r0 + SUBLANES:r0 + hr],
                               l[r0 + SUBLANES:r0 + hr], lam, gs_ref[...], lam_init)
            o_ref[h] = out.astype(o_ref.dtype)


def _diff_decode(page_table, q, kt_new, v_new, lams, g_sub, cache_kt, cache_v, layer, n_slots, lam_init):
    nb, n_pages = page_table.shape
    n_heads = q.shape[1]
    n_steps = n_pages // n_slots
    kt_blk, v_blk = (n_heads, LANES, LANES), (LANES, n_heads, LANES)
    page_of = lambda j, p: j * n_slots + p
    per_b = lambda b, j, pt: (b, 0, 0, 0)
    vec64 = pl.BlockSpec((1, LANES // 2), lambda b, j, pt: (0, 0))
    rows = n_heads * 2 * SUBLANES
    grid_spec = pltpu.PrefetchScalarGridSpec(
        num_scalar_prefetch=1,
        grid=(nb, n_steps),
        in_specs=[pl.BlockSpec((None, n_heads, SUBLANES, LANES), per_b),
                  pl.BlockSpec((None,) + kt_blk, per_b),
                  pl.BlockSpec((None,) + v_blk, per_b),
                  vec64, vec64, vec64, vec64, pl.BlockSpec((1, LANES), lambda b, j, pt: (0, 0))]
                 + _page_specs(kt_blk, layer, n_slots, page_of) + _page_specs(v_blk, layer, n_slots, page_of),
        out_specs=pl.BlockSpec((None, n_heads, SUBLANES, LANES), per_b),
        scratch_shapes=[pltpu.VMEM((rows, 1), F32), pltpu.VMEM((rows, 1), F32), pltpu.VMEM((rows, LANES), F32)])
    return pl.pallas_call(
        functools.partial(_diff_dec_kernel, n_slots=n_slots, n_heads=n_heads, n_steps=n_steps,
                          past=n_pages * LANES, lam_init=lam_init),
        grid_spec=grid_spec,
        out_shape=jax.ShapeDtypeStruct((nb, n_heads, SUBLANES, LANES), BF16),
        compiler_params=_cparams(2, 48),
        name="diff_decode_attn",
    )(page_table, q, kt_new, v_new, *lams, g_sub, *([cache_kt] * n_slots), *([cache_v] * n_slots))


def _sb_dec_kernel(pt_ref, q_ref, kn_ref, vn_ref, *rest, n_slots, n_heads, n_steps):
    k_refs, v_refs = rest[:n_slots], rest[n_slots:2 * n_slots]
    o_ref, later_sc, acc_sc = rest[2 * n_slots:]
    j = pl.program_id(1)
    rows = n_heads * SUBLANES
    u = _suffix_ones(LANES)

    def pages(k_of, v_of, n, valid):
        z = jnp.concatenate(
            [lax.dot_general(q_ref[h], k_of(pg, h).astype(BF16), NT_DIMS, preferred_element_type=F32)
             for pg in range(n) for h in range(n_heads)], axis=0)
        l1mb = -jnp.maximum(z, 0.0) - jnp.log(1.0 + jnp.exp(-jnp.abs(z)))
        if valid is not None:
            l1mb = jnp.where(valid, l1mb, 0.0)
        page_sum = jnp.sum(l1mb, axis=-1, keepdims=True)
        later = later_sc[...]
        laters = []
        for pg in range(n):
            laters.append(later)
            later = later + page_sum[pg * rows:(pg + 1) * rows]
        after = _suffix_sum(l1mb, u) + jnp.concatenate(laters, axis=0)
        a = jnp.exp(l1mb + z + after)
        if valid is not None:
            a = jnp.where(valid, a, 0.0)
        ab = a.astype(BF16)
        pv = []
        for h in range(n_heads):
            tot = None
            for pg in range(n):
                r0 = pg * rows + h * SUBLANES
                part = jnp.dot(ab[r0:r0 + SUBLANES], v_of(pg, h).astype(BF16), preferred_element_type=F32)
                tot = part if tot is None else tot + part
            pv.append(tot)
        acc_sc[...] += jnp.concatenate(pv, axis=0)
        later_sc[...] = later

    @pl.when(j == 0)
    def _():
        later_sc[...] = jnp.zeros((rows, 1), F32)
        acc_sc[...] = jnp.zeros((rows, LANES), F32)
        col, t = _new_token_mask(rows)
        pages(lambda pg, h: kn_ref[h], lambda pg, h: vn_ref[h], 1, col < t)

    pages(lambda pg, h: k_refs[pg][h], lambda pg, h: v_refs[pg][h], n_slots, None)

    @pl.when(j == n_steps - 1)
    def _():
        o_ref[...] = acc_sc[...].reshape(n_heads, SUBLANES, LANES).astype(o_ref.dtype)


def _sb_decode(page_table, q, k_new, v_new, cache_k, cache_v, layer, n_slots):
    nb, n_pages = page_table.shape
    n_heads = q.shape[1]
    n_steps = n_pages // n_slots
    page_blk = (n_heads, LANES, LANES)
    page_of = lambda j, p: n_pages - 1 - (j * n_slots + p)
    per_b = lambda b, j, pt: (b, 0, 0, 0)
    grid_spec = pltpu.PrefetchScalarGridSpec(
        num_scalar_prefetch=1,
        grid=(nb, n_steps),
        in_specs=[pl.BlockSpec((None, n_heads, SUBLANES, LANES), per_b),
                  pl.BlockSpec((None,) + page_blk, per_b),
                  pl.BlockSpec((None,) + page_blk, per_b)]
                 + _page_specs(page_blk, layer, n_slots, page_of) + _page_specs(page_blk, layer, n_slots, page_of),
        out_specs=pl.BlockSpec((None, n_heads, SUBLANES, LANES), per_b),
        scratch_shapes=[pltpu.VMEM((n_heads * SUBLANES, 1), F32), pltpu.VMEM((n_heads * SUBLANES, LANES), F32)])
    return pl.pallas_call(
        functools.partial(_sb_dec_kernel, n_slots=n_slots, n_heads=n_heads, n_steps=n_steps),
        grid_spec=grid_spec,
        out_shape=jax.ShapeDtypeStruct((nb, n_heads, SUBLANES, LANES), BF16),
        compiler_params=_cparams(2, 48),
        name="sb_decode_attn",
    )(page_table, q, k_new, v_new, *([cache_k] * n_slots), *([cache_v] * n_slots))


def _merge_kernel(oa, ob, oc, wa, wb, wc, g0, g1, g2, o_ref):
    pa = jnp.dot(oa[...], wa[...].astype(BF16), preferred_element_type=F32)
    pb = jnp.dot(ob[...], wb[...].astype(BF16), preferred_element_type=F32)
    pc = jnp.dot(oc[...], wc[...].astype(BF16), preferred_element_type=F32)
    o_ref[...] = (g0[...] * pa + g1[...] * pb + g2[...] * pc).astype(o_ref.dtype)


def _merge(oa, ob, oc, wa, wb, wc, layer, gates, tm, tn):
    m = oa.shape[0]
    d_model = wa.shape[2]
    nj = d_model // tn
    a_spec = lambda a: pl.BlockSpec((tm, a.shape[1]), lambda i, j: (i, 0))
    w_spec = lambda w: pl.BlockSpec((None, w.shape[1], tn), lambda i, j: (layer, 0, j))
    g_spec = lambda c: pl.BlockSpec((tm, tn), lambda i, j: (i, c * nj + j))
    return pl.pallas_call(
        _merge_kernel,
        grid=(m // tm, nj),
        in_specs=[a_spec(oa), a_spec(ob), a_spec(oc), w_spec(wa), w_spec(wb), w_spec(wc),
                  g_spec(0), g_spec(1), g_spec(2)],
        out_specs=pl.BlockSpec((tm, tn), lambda i, j: (i, j)),
        out_shape=jax.ShapeDtypeStruct((m, d_model), BF16),
        compiler_params=_cparams(2, 48),
        name="branch_merge",
    )(oa, ob, oc, wa, wb, wc, gates, gates, gates)


def _project(h, g, w_t, layer, name, col0, ncols, tn, epilogue, extras=(), extra_specs=(), heads=None, dtype=F32,
             stack=None, into=None):
    m, d_model = h.shape
    if heads is not None:
        shape, spec = _heads_out(g, heads, tn // LANES, dtype, stack)
    else:
        shape = jax.ShapeDtypeStruct((m, ncols), dtype)
        spec = pl.BlockSpec((g.tm, tn), lambda i, j, k: (i, j))
    return _mm(name, h, w_t, layer, col0, ncols, g.tm, tn, d_model, epilogue, list(extras), list(extra_specs),
               shape, spec, w_rows_are_outputs=True, into=into)


def _vec_spec(n):
    return pl.BlockSpec((1, n), lambda i, j, k: (0, 0))


def _layer_front(x, mod, g, p, dims, stack=None, prev=None):
    fox_w, dqk_w, dv_w, sb_w, d_model = dims
    kv = lambda name: dict(stack=stack, into=None if prev is None else prev[name])
    fh, sh = fox_w // LANES, sb_w // LANES
    h = _prenorm(x, p["g_norm1"], mod, g, 1, 0)
    tn = 512
    tn_heads = _pick(fox_w, 768) if fox_w == sb_w else tn
    sc128 = 1.0 / math.sqrt(LANES)
    rms = lambda scale: functools.partial(_epi_heads_rms, g=g, scale=scale)
    plain = lambda scale: functools.partial(_epi_heads_plain, g=g, scale=scale)
    half = lambda scale: functools.partial(_epi_rms_half, scale=scale)
    out = {}
    proj = functools.partial(_project, h, g, p["w_in_t"], p["layer"])
    c = 0
    out["qa"] = proj("proj_fox_q", c, fox_w, tn_heads,rms(sc128), [p["g_q_fox"]], [_vec_spec(LANES)], heads=fh, dtype=BF16); c += fox_w
    out["ka"] = proj("proj_fox_k", c, fox_w, tn_heads,rms(1.0), [p["g_k_fox"]], [_vec_spec(LANES)], heads=fh, **kv("ka")); c += fox_w
    out["va"] = proj("proj_fox_v", c, fox_w, tn_heads,plain(1.0), heads=fh, **kv("va")); c += fox_w
    out["qb"] = proj("proj_diff_q", c, dqk_w, tn, half(0.125), [p["g_q_diff"]], [_vec_spec(LANES)], dtype=BF16); c += dqk_w
    out["kb"] = proj("proj_diff_k", c, dqk_w, tn, half(1.0), [p["g_k_diff"]], [_vec_spec(LANES)]); c += dqk_w
    out["vb"] = proj("proj_diff_v", c, dv_w, tn, _epi_plain2d); c += dv_w
    out["qc"] = proj("proj_sb_q", c, sb_w, tn_heads,plain(sc128), heads=sh, dtype=BF16); c += sb_w
    out["kc"] = proj("proj_sb_k", c, sb_w, tn_heads,plain(1.0), heads=sh, **kv("kc")); c += sb_w
    out["vc"] = proj("proj_sb_v", c, sb_w, tn_heads,plain(1.0), heads=sh, **kv("vc")); c += sb_w
    out["gates"] = proj("proj_gates", c, 3 * d_model, _pick(3 * d_model, 1024), _epi_sigmoid, dtype=BF16)
    out["lf_t"] = _lf_proj(p["w_fa_t"], h, p["b_fa"])
    return out


def _layer_back(x, mod, g, p, oa, ob, oc, gates):
    m, d_model = x.shape
    layer = p["layer"]
    merged = _merge(oa, ob, oc, p["w_br_fox"], p["w_br_diff"], p["w_br_sb"], layer, gates, g.tm, 256)
    tn = 512
    xspec = pl.BlockSpec((g.tm, tn), lambda i, j, k: (i, j))
    oshape = jax.ShapeDtypeStruct((m, d_model), F32)
    x1 = _mm("out_proj", merged, p["w_out"], layer, 0, d_model, g.tm, tn, d_model, _epi_residual,
             [x, mod], [xspec, _mod_spec(g, 2, d_model, tn)], oshape, xspec)
    h2 = _prenorm(x1, p["g_norm2"], mod, g, 4, 3)
    d_ff = p["w_up"].shape[2]
    f = _mm("mlp_up", h2, p["w_up"], layer, 0, d_ff, g.tm, tn, d_model, _epi_relu2, [], [],
            jax.ShapeDtypeStruct((m, d_ff), BF16), xspec)
    tn2 = _pick(d_model, 1024)
    xspec2 = pl.BlockSpec((g.tm, tn2), lambda i, j, k: (i, j))
    return _mm("mlp_down", f, p["w_down"], layer, 0, d_model, g.tm, tn2, _pick(d_ff, 2048), _epi_residual,
               [x1, mod], [xspec2, _mod_spec(g, 5, d_model, tn2)], oshape, xspec2)


def kernel(x_prompt, x_sample, cache_fox_k, cache_fox_v, cache_fox_logf, cache_diff_k, cache_diff_v, cache_sb_k, cache_sb_v, page_table, c_prompt, c_sample, w_ada, b_ada, g_norm1, w_in, b_fgate, g_q_fox, g_k_fox, g_q_diff, g_k_diff, lambda_q1, lambda_k1, lambda_q2, lambda_k2, g_sub_diff, w_br_fox, w_br_diff, w_br_sb, w_out, g_norm2, w_up, w_down):
    n_layers = w_ada.shape[0]
    nbp, seq, d_model = x_prompt.shape
    nbs, dseq, _ = x_sample.shape
    assert dseq == SUBLANES, "decode kernels hold the new tokens of a sequence in one sublane group"
    n_pages = page_table.shape[1]
    n_pool, page_size, fh, fd = cache_fox_k.shape[1:]
    dh, dqk = cache_diff_k.shape[3], cache_diff_k.shape[5]
    sh = cache_sb_k.shape[3]
    assert page_size == LANES and fd == LANES and 2 * dqk == LANES and cache_sb_k.shape[4] == LANES
    fox_w, dqk_w, dv_w, sb_w = fh * LANES, dh * LANES, dh * LANES, sh * LANES
    dims = (fox_w, dqk_w, dv_w, sb_w, d_model)
    past = n_pages * page_size
    mp, ms = nbp * seq, nbs * dseq

    gp = Group(nbp, seq, _pick(seq, 1024), False)
    gs = Group(nbs, dseq, ms, True)
    tq_fox, tq_diff, tq_sb = _pick(seq, 1024), _pick(seq, 512), _pick(seq, 512)
    tk_fox, tk_diff, tk_sb = _pick(seq, 512), _pick(seq, 512), _pick(seq, 256)
    n_slots = max(s for s in (8, 4, 2, 1) if n_pages % s == 0)

    n_c = nbp + nbs
    c_rows = -(-n_c // SUBLANES) * SUBLANES
    c_all = jnp.concatenate([c_prompt, c_sample, jnp.zeros((c_rows - n_c, d_model), F32)], axis=0)
    mod_all = _ada(c_all, w_ada, b_ada)

    fox_k_hm = jnp.transpose(cache_fox_k, (0, 1, 3, 2, 4))
    fox_v_hm = jnp.transpose(cache_fox_v, (0, 1, 3, 2, 4))
    sb_k_hm = jnp.transpose(cache_sb_k, (0, 1, 3, 2, 4))
    sb_v_hm = jnp.transpose(cache_sb_v, (0, 1, 3, 2, 4))
    diff_kt = jnp.transpose(cache_diff_k, (0, 1, 3, 4, 5, 2)).reshape(n_layers, n_pool, dh, LANES, page_size)
    logf_t = jnp.transpose(cache_fox_logf, (0, 3, 1, 2))
    pool_pad = -(-n_pool // SUBLANES) * SUBLANES

    c_fa = 3 * fox_w
    n_fa = fh
    yp, ys = x_prompt.reshape(mp, d_model), x_sample.reshape(ms, d_model)
    outs_p, outs_s = [], []
    w_fa_rows = jnp.transpose(w_in, (2, 0, 1))[c_fa:c_fa + n_fa]
    w_in_l = _w_in_by_layer(w_in, c_fa, n_fa, 256)
    w_down_bf = w_down.astype(BF16)
    for l in range(n_layers):
        lam_init = 0.8 - 0.6 * math.exp(-0.3 * l)
        pick_l = (jnp.arange(n_layers) == l)[None, :, None]
        p = {
            "layer": l,
            "w_in_t": w_in_l,
            "w_fa_t": jnp.pad(jnp.sum(jnp.where(pick_l, w_fa_rows, 0.0), axis=1), ((0, HEAD_PAD - n_fa), (0, 0))),
            "b_fa": jnp.pad(b_fgate[l], (0, HEAD_PAD - n_fa)).reshape(HEAD_PAD, 1),
            "g_norm1": g_norm1[l], "g_norm2": g_norm2[l],
            "g_q_fox": g_q_fox[l].reshape(1, LANES), "g_k_fox": g_k_fox[l].reshape(1, LANES),
            "g_q_diff": jnp.tile(g_q_diff[l], 2).reshape(1, LANES), "g_k_diff": jnp.tile(g_k_diff[l], 2).reshape(1, LANES),
            "w_br_fox": w_br_fox, "w_br_diff": w_br_diff, "w_br_sb": w_br_sb,
            "w_out": w_out, "w_up": w_up, "w_down": w_down_bf,
        }
        lams = [v[l].reshape(1, dqk) for v in (lambda_q1, lambda_k1, lambda_q2, lambda_k2)]
        g_sub = g_sub_diff[l].reshape(1, LANES)

        mod_p = mod_all[l, :nbp].reshape(nbp, 1, -1)
        fr = _layer_front(yp, mod_p, gp, p, dims, stack=(n_layers, l), prev=fr if l else None)
        lf_p = fr["lf_t"]
        x_e = lf_p.reshape(HEAD_PAD, nbp, seq // LANES, LANES).transpose(1, 2, 0, 3)
        e_p = _ecum(x_e).transpose(0, 2, 1, 3).reshape(nbp, HEAD_PAD, 1, seq)
        oa = _fox_prompt(fr["qa"], fr["ka"], fr["va"], l, e_p, tq_fox, tk_fox)
        ob = _diff_prompt(fr["qb"], fr["kb"], fr["vb"], lams, g_sub, nbp, seq, tq_diff, tk_diff, lam_init)
        oc = _sb_prompt(fr["qc"], fr["kc"], fr["vc"], l, tq_sb, tk_sb)
        yp = _layer_back(yp, mod_p, gp, p, oa, ob, oc, fr["gates"])
        hm_out = lambda a: jnp.transpose(a, (0, 2, 1, 3))
        outs_p.append((
            lf_p[:fh].reshape(fh, nbp, seq).transpose(1, 2, 0),
            fr["kb"].reshape(nbp, seq, dh, 2, dqk), fr["vb"].reshape(nbp, seq, dh, LANES)))

        mod_s = jnp.repeat(mod_all[l, nbp:nbp + nbs], dseq, axis=0)
        fs = _layer_front(ys, mod_s, gs, p, dims)
        lf_s = fs["lf_t"]
        lf_new = jnp.pad(lf_s.reshape(HEAD_PAD, nbs, dseq).transpose(1, 0, 2), ((0, 0), (0, 0), (0, LANES - dseq)))
        table = jnp.pad(logf_t[l], ((0, 0), (0, pool_pad - n_pool), (0, 0))).reshape(fh, pool_pad // SUBLANES, SUBLANES, LANES)
        e_s = _ecum(_lf_gather(page_table, table, lf_new))
        pad_keys = lambda a: jnp.pad(a, ((0, 0), (0, 0), (0, LANES - dseq), (0, 0)))
        oa = _fox_decode(page_table, fs["qa"], e_s, pad_keys(fs["ka"]), pad_keys(fs["va"]), fox_k_hm, fox_v_hm, l, n_slots)
        qb_hm = fs["qb"].reshape(nbs, dseq, dh, LANES).transpose(0, 2, 1, 3)
        kb_t = fs["kb"].reshape(nbs, dseq, dh, LANES).transpose(0, 2, 3, 1)
        kt_new = jnp.pad(kb_t, ((0, 0), (0, 0), (0, 0), (0, LANES - dseq)))
        vb_new = jnp.pad(fs["vb"].reshape(nbs, dseq, dh, LANES), ((0, 0), (0, LANES - dseq), (0, 0), (0, 0)))
        ob = _diff_decode(page_table, qb_hm, kt_new, vb_new, lams, g_sub, diff_kt, cache_diff_v, l, n_slots, lam_init)
        oc = _sb_decode(page_table, fs["qc"], pad_keys(fs["kc"]), pad_keys(fs["vc"]), sb_k_hm, sb_v_hm, l, n_slots)
        rows2d = lambda a: jnp.transpose(a, (0, 2, 1, 3)).reshape(ms, -1)
        ys = _layer_back(ys, mod_s, gs, p, rows2d(oa), rows2d(ob), rows2d(oc), fs["gates"])
        outs_s.append((
            hm_out(fs["ka"]), hm_out(fs["va"]), lf_s[:fh].reshape(fh, nbs, dseq).transpose(1, 2, 0),
            fs["kb"].reshape(nbs, dseq, dh, 2, dqk), fs["vb"].reshape(nbs, dseq, dh, LANES),
            hm_out(fs["kc"]), hm_out(fs["vc"])))

    stack = lambda outs: tuple(jnp.stack([o[i] for o in outs]) for i in range(len(outs[0])))
    p_lf, p_dk, p_dv = stack(outs_p)
    hm_stacked = lambda a: jnp.transpose(a, (0, 1, 3, 2, 4))
    prompt_rows = (hm_stacked(fr["ka"]), hm_stacked(fr["va"]), p_lf, p_dk, p_dv, hm_stacked(fr["kc"]), hm_stacked(fr["vc"]))
    return (yp.reshape(nbp, seq, d_model), ys.reshape(nbs, dseq, d_model)) + prompt_rows + stack(outs_s)
```
